```python
import math
import jax
import jax.numpy as jnp
from jax import lax
import numpy as np


D_MODEL = 2048
BATCH = 16
SEQ = 2048
DEPTH = 4

GRID_W = 64
CTX_LEN = 256
HEAD_DIM = 64
N_BRANCH = 4
MIX_W = D_MODEL // N_BRANCH
NA_HEADS = MIX_W // HEAD_DIM
NA_ROWS = 8
NA_COLS = 16
DIFF_HEADS = MIX_W // (2 * HEAD_DIM)
DIFF_V_DIM = 2 * HEAD_DIM
SWA_HEADS = MIX_W // HEAD_DIM
SWA_KV_HEADS = 2
SWA_GROUP = SWA_HEADS // SWA_KV_HEADS
SWA_WINDOW = 128
POOL_WINDOWS = (2, 4, 8, 16)
POOL_GROUP_W = MIX_W // len(POOL_WINDOWS)
D_FF = -(-(8 * D_MODEL) // (3 * 256)) * 256
Q_BLOCK = 128
ROPE_BASE = 10000.0
NORM_EPS = 1e-6
MASK_VALUE = -1e30
ATTN_SCALE = HEAD_DIM ** -0.5
PROJ_SIZES = (MIX_W, MIX_W, MIX_W, MIX_W, MIX_W, MIX_W, MIX_W, SWA_KV_HEADS * HEAD_DIM, SWA_KV_HEADS * HEAD_DIM, MIX_W, N_BRANCH * D_MODEL)
PROJ_W = sum(PROJ_SIZES)

kernel_name = 'hybrid_dit_prefix_block'


def rmsnorm(x, g):
    xf = x.astype(jnp.float32)
    y = xf * lax.rsqrt(jnp.mean(xf * xf, axis=-1, keepdims=True) + NORM_EPS)
    return (y * g.astype(jnp.float32)).astype(x.dtype)


def modulate(x, shift, scale):
    return x * (1 + scale) + shift


def split_heads(t, shape):
    return t.reshape(t.shape[:-1] + shape)


def split_cols(t):
    offsets = []
    acc = 0
    for s in PROJ_SIZES[:-1]:
        acc += s
        offsets.append(acc)
    return jnp.split(t, offsets, axis=-1)


def rope_tables(seq):
    t = jnp.arange(seq)
    pos = jnp.stack([t // GRID_W, t % GRID_W], axis=-1).astype(jnp.float32)
    n_freq = HEAD_DIM // 4
    inv = ROPE_BASE ** (-jnp.arange(n_freq, dtype=jnp.float32) / n_freq)
    ang = pos[:, :, None] * inv
    return jnp.cos(ang), jnp.sin(ang)


def rope2d(x, cos, sin):
    n_mid = x.ndim - 3
    cs_shape = (cos.shape[0],) + (1,) * n_mid + cos.shape[1:]
    cos = cos.reshape(cs_shape).astype(x.dtype)
    sin = sin.reshape(cs_shape).astype(x.dtype)
    xs = x.reshape(x.shape[:-1] + (2, 2, HEAD_DIM // 4))
    x0, x1 = xs[..., 0, :], xs[..., 1, :]
    out = jnp.stack([x0 * cos - x1 * sin, x1 * cos + x0 * sin], axis=-2)
    return out.reshape(x.shape)


def neighbourhood_attention(q, k, v, kc, vc, rpb):
    B_, S, H, dh = q.shape
    rows = S // GRID_W
    kr = min(NA_ROWS, rows)
    r = jnp.arange(rows)
    row_idx = jnp.clip(r - kr // 2, 0, rows - kr)[:, None] + jnp.arange(kr)[None, :]
    dr = row_idx - r[:, None] + (NA_ROWS - 1)
    col = jnp.arange(GRID_W)
    col_start = jnp.clip(col - NA_COLS // 2, 0, GRID_W - NA_COLS)
    col_mask = (col[None, :] >= col_start[:, None]) & (col[None, :] < col_start[:, None] + NA_COLS)
    dc = jnp.clip(col[None, :] - col[:, None], -(NA_COLS - 1), NA_COLS - 1) + (NA_COLS - 1)
    bias = rpb[:, dr[:, None, :, None], dc[None, :, None, :]]
    qg = q.reshape(B_, rows, GRID_W, H, dh)
    kg = k.reshape(B_, rows, GRID_W, H, dh)[:, row_idx]
    vg = v.reshape(B_, rows, GRID_W, H, dh)[:, row_idx]
    s = jnp.einsum('brqhd,brkwhd->bhrqkw', qg, kg).astype(jnp.float32) * ATTN_SCALE + bias[None].astype(jnp.float32)
    s = jnp.where(col_mask[:, None, :], s, MASK_VALUE)
    n_lat = kr * GRID_W
    s = s.reshape(B_, H, rows, GRID_W, n_lat)
    sc = jnp.einsum('brqhd,bchd->bhrqc', qg, kc).astype(jnp.float32) * ATTN_SCALE
    p = jax.nn.softmax(jnp.concatenate([s, sc], axis=-1), axis=-1).astype(v.dtype)
    p_lat = p[..., :n_lat].reshape(B_, H, rows, GRID_W, kr, GRID_W)
    o = jnp.einsum('bhrqkw,brkwhd->brqhd', p_lat, vg) + jnp.einsum('bhrqc,bchd->brqhd', p[..., n_lat:], vc)
    return o.reshape(B_, S, H * dh)


def diff_attention_latent(q, k, v, kc, vc, lam):
    B_, S = q.shape[:2]
    nb = S // Q_BLOCK
    qb = jnp.moveaxis(q.reshape((B_, nb, Q_BLOCK) + q.shape[2:]), 1, 0)

    def block(qblk):
        s = jnp.einsum('bqhid,bkhid->bhiqk', qblk, k)
        sc = jnp.einsum('bqhid,bchid->bhiqc', qblk, kc)
        p = jax.nn.softmax(jnp.concatenate([s, sc], axis=-1).astype(jnp.float32) * ATTN_SCALE, axis=-1)
        a = (p[:, :, 0] - lam * p[:, :, 1]).astype(v.dtype)
        return jnp.einsum('bhqk,bkhe->bqhe', a[..., :S], v) + jnp.einsum('bhqc,bche->bqhe', a[..., S:], vc)

    o = lax.map(block, qb)
    return jnp.moveaxis(o, 0, 1).reshape((B_, S) + v.shape[2:])


def diff_attention_ctx(q, k, v, lam):
    s = jnp.einsum('bqhid,bkhid->bhiqk', q, k).astype(jnp.float32) * ATTN_SCALE
    p = jax.nn.softmax(s, axis=-1)
    a = (p[:, :, 0] - lam * p[:, :, 1]).astype(v.dtype)
    return jnp.einsum('bhqk,bkhe->bqhe', a, v)


def diff_output(o, g, lam_init):
    return (rmsnorm(o, g) * (1.0 - lam_init)).reshape(o.shape[:2] + (-1,))


def window_attention(q, k, v, kc, vc, sink):
    B_, S = q.shape[:2]
    W = SWA_WINDOW
    nb = S // W

    def bands(t):
        tp = jnp.pad(t, ((0, 0), (W, W), (0, 0), (0, 0))).reshape((B_, nb + 2, W) + t.shape[2:])
        return jnp.concatenate([tp[:, :-2], tp[:, 1:-1], tp[:, 2:]], axis=2)

    kb, vb = bands(k), bands(v)
    qb = q.reshape((B_, nb, W) + q.shape[2:])
    i = jnp.arange(W)
    j = jnp.arange(3 * W)
    n = jnp.arange(nb)
    rel = j[None, :] - W - i[:, None]
    kpos = n[:, None] * W - W + j[None, :]
    valid = (jnp.abs(rel) <= W)[None] & ((kpos >= 0) & (kpos < S))[:, None, :]
    s = jnp.einsum('bnqkgd,bnjkd->bnkgqj', qb, kb).astype(jnp.float32) * ATTN_SCALE
    s = jnp.where(valid[None, :, None, None], s, MASK_VALUE)
    sc = jnp.einsum('bnqkgd,bckd->bnkgqc', qb, kc).astype(jnp.float32) * ATTN_SCALE
    snk = jnp.broadcast_to(sink.astype(jnp.float32)[None, None, :, :, None, None], s.shape[:-1] + (1,))
    p = jax.nn.softmax(jnp.concatenate([s, sc, snk], axis=-1), axis=-1).astype(v.dtype)
    n_lat = 3 * W
    n_ctx = kc.shape[1]
    o = jnp.einsum('bnkgqj,bnjkd->bnqkgd', p[..., :n_lat], vb) + jnp.einsum('bnkgqc,bckd->bnqkgd', p[..., n_lat:n_lat + n_ctx], vc)
    return o.reshape(B_, S, -1)


def ctx_attention(q, k, v, sink=None):
    B_, L = q.shape[:2]
    s = jnp.einsum('bqkgd,bckd->bkgqc', q, k).astype(jnp.float32) * ATTN_SCALE
    if sink is not None:
        snk = jnp.broadcast_to(sink.astype(jnp.float32)[None, :, :, None, None], s.shape[:-1] + (1,))
        s = jnp.concatenate([s, snk], axis=-1)
    p = jax.nn.softmax(s, axis=-1)[..., :L].astype(v.dtype)
    o = jnp.einsum('bkgqc,bckd->bqkgd', p, v)
    return o.reshape(B_, L, -1)


def pool_mixer(u, w_grp, scale):
    B_, T, _ = u.shape
    ug = u.reshape(B_, T, len(POOL_WINDOWS), POOL_GROUP_W).astype(jnp.float32)
    cs = jnp.concatenate([jnp.zeros_like(ug[:, :1]), jnp.cumsum(ug, axis=1)], axis=1)
    t = jnp.arange(T)
    means = []
    for g, w in enumerate(POOL_WINDOWS):
        lo = jnp.clip(t - w // 2, 0, T)
        hi = jnp.clip(t - w // 2 + w, 0, T)
        cnt = (hi - lo).astype(jnp.float32)[None, :, None]
        means.append((cs[:, hi, g] - cs[:, lo, g]) / cnt)
    pooled = jnp.stack(means, axis=2)
    d = (pooled - ug).astype(u.dtype)
    mixed = jnp.einsum('btgc,gce->btge', d, w_grp)
    return mixed.reshape(B_, T, MIX_W) * scale


def merge_branches(outs, gate_logits, w_br, w_o):
    B_, T = gate_logits.shape[:2]
    gates = jax.nn.sigmoid(gate_logits).reshape(B_, T, N_BRANCH, D_MODEL)
    acc = gates[:, :, 0] * (outs[0] @ w_br[0])
    for n in range(1, N_BRANCH):
        acc = acc + gates[:, :, n] * (outs[n] @ w_br[n])
    return acc @ w_o


def swiglu(h, wg, wu, wd):
    return (jax.nn.silu(h @ wg) * (h @ wu)) @ wd


def setup_inputs(seed: int = 0) -> dict:
    key = jax.random.key(seed)
    ks = jax.random.split(key, 32)
    f32 = jnp.float32
    L = DEPTH
    D = D_MODEL

    def nrm(k, shape, scale):
        return jax.random.normal(k, shape, f32) * scale

    return {
        'x': nrm(ks[0], (BATCH, SEQ, D), 1.0),
        'c': nrm(ks[1], (BATCH, D), 1.0),
        'ctx': nrm(ks[2], (BATCH, CTX_LEN, D), 1.0),
        'c_ctx': nrm(ks[3], (D,), 1.0),
        'w_ada': nrm(ks[4], (L, D, 6 * D), 0.5 * D ** -0.5),
        'b_ada': nrm(ks[5], (L, 6 * D), 0.02),
        'norm_mix': 1.0 + nrm(ks[6], (L, D), 0.02),
        'norm_ffn': 1.0 + nrm(ks[7], (L, D), 0.02),
        'w_in': nrm(ks[8], (L, D, PROJ_W), D ** -0.5),
        'a_q_norm': 1.0 + nrm(ks[9], (L, HEAD_DIM), 0.02),
        'a_k_norm': 1.0 + nrm(ks[10], (L, HEAD_DIM), 0.02),
        'a_rpb': nrm(ks[11], (L, NA_HEADS, 2 * NA_ROWS - 1, 2 * NA_COLS - 1), 0.1),
        'b_q_norm': 1.0 + nrm(ks[12], (L, HEAD_DIM), 0.02),
        'b_k_norm': 1.0 + nrm(ks[13], (L, HEAD_DIM), 0.02),
        'b_lam_q1': nrm(ks[14], (L, HEAD_DIM), 0.1),
        'b_lam_k1': nrm(ks[15], (L, HEAD_DIM), 0.1),
        'b_lam_q2': nrm(ks[16], (L, HEAD_DIM), 0.1),
        'b_lam_k2': nrm(ks[17], (L, HEAD_DIM), 0.1),
        'b_subln': 1.0 + nrm(ks[18], (L, DIFF_V_DIM), 0.02),
        'c_q_norm': 1.0 + nrm(ks[19], (L, HEAD_DIM), 0.02),
        'c_k_norm': 1.0 + nrm(ks[20], (L, HEAD_DIM), 0.02),
        'c_sink': nrm(ks[21], (L, SWA_HEADS), 0.5),
        'd_w': nrm(ks[22], (L, len(POOL_WINDOWS), POOL_GROUP_W, POOL_GROUP_W), POOL_GROUP_W ** -0.5),
        'd_scale': 1.0 + nrm(ks[23], (L, MIX_W), 0.1),
        'w_branch': nrm(ks[24], (L, N_BRANCH, MIX_W, D), MIX_W ** -0.5),
        'w_out': nrm(ks[25], (L, D, D), D ** -0.5),
        'w_ffn_gate': nrm(ks[26], (L, D, D_FF), D ** -0.5),
        'w_ffn_up': nrm(ks[27], (L, D, D_FF), D ** -0.5),
        'w_ffn_down': nrm(ks[28], (L, D_FF, D), D_FF ** -0.5),
    }


def reference(x, c, ctx, c_ctx, w_ada, b_ada, norm_mix, norm_ffn, w_in, a_q_norm, a_k_norm, a_rpb, b_q_norm, b_k_norm, b_lam_q1, b_lam_k1, b_lam_q2, b_lam_k2, b_subln, c_q_norm, c_k_norm, c_sink, d_w, d_scale, w_branch, w_out, w_ffn_gate, w_ffn_up, w_ffn_down):
    D = x.shape[-1]
    cos, sin = rope_tables(x.shape[1])
    c_act = jax.nn.silu(c)
    cctx_act = jax.nn.silu(c_ctx)
    for l in range(DEPTH):
        last = l == DEPTH - 1
        mod = c_act @ w_ada[l] + b_ada[l]
        sh_mix, sc_mix, g_mix, sh_ffn, sc_ffn, g_ffn = [m[:, None, :] for m in jnp.split(mod, 6, axis=-1)]
        n_ctx_mod = 2 if last else 6
        mod_c = jnp.split(cctx_act @ w_ada[l, :, :n_ctx_mod * D] + b_ada[l, :n_ctx_mod * D], n_ctx_mod, axis=-1)

        h = modulate(rmsnorm(x, norm_mix[l]), sh_mix, sc_mix)
        hc = modulate(rmsnorm(ctx, norm_mix[l]), mod_c[0], mod_c[1])
        aq, ak, av, bq, bk, bv, sq, sk, sv, du, gl = split_cols(h @ w_in[l])
        if last:
            wp = split_cols(w_in[l])
            akc, avc, bkc, bvc, skc, svc = [hc @ wp[i] for i in (1, 2, 4, 5, 7, 8)]
        else:
            aqc, akc, avc, bqc, bkc, bvc, sqc, skc, svc, duc, glc = split_cols(hc @ w_in[l])

        lam_init = 0.8 - 0.6 * math.exp(-0.3 * l)
        lam = (jnp.exp(jnp.sum(b_lam_q1[l].astype(jnp.float32) * b_lam_k1[l].astype(jnp.float32)))
               - jnp.exp(jnp.sum(b_lam_q2[l].astype(jnp.float32) * b_lam_k2[l].astype(jnp.float32))) + lam_init)
        sink = c_sink[l].reshape(SWA_KV_HEADS, SWA_GROUP)

        ka_c = rmsnorm(split_heads(akc, (NA_HEADS, HEAD_DIM)), a_k_norm[l])
        va_c = split_heads(avc, (NA_HEADS, HEAD_DIM))
        kb_c = rmsnorm(split_heads(bkc, (DIFF_HEADS, 2, HEAD_DIM)), b_k_norm[l])
        vb_c = split_heads(bvc, (DIFF_HEADS, DIFF_V_DIM))
        ks_c = rmsnorm(split_heads(skc, (SWA_KV_HEADS, HEAD_DIM)), c_k_norm[l])
        vs_c = split_heads(svc, (SWA_KV_HEADS, HEAD_DIM))

        o_a = neighbourhood_attention(
            rmsnorm(split_heads(aq, (NA_HEADS, HEAD_DIM)), a_q_norm[l]),
            rmsnorm(split_heads(ak, (NA_HEADS, HEAD_DIM)), a_k_norm[l]),
            split_heads(av, (NA_HEADS, HEAD_DIM)), ka_c, va_c, a_rpb[l])
        o_b = diff_output(diff_attention_latent(
            rope2d(rmsnorm(split_heads(bq, (DIFF_HEADS, 2, HEAD_DIM)), b_q_norm[l]), cos, sin),
            rope2d(rmsnorm(split_heads(bk, (DIFF_HEADS, 2, HEAD_DIM)), b_k_norm[l]), cos, sin),
            split_heads(bv, (DIFF_HEADS, DIFF_V_DIM)), kb_c, vb_c, lam), b_subln[l], lam_init)
        o_s = window_attention(
            rope2d(rmsnorm(split_heads(sq, (SWA_KV_HEADS, SWA_GROUP, HEAD_DIM)), c_q_norm[l]), cos, sin),
            rope2d(rmsnorm(split_heads(sk, (SWA_KV_HEADS, HEAD_DIM)), c_k_norm[l]), cos, sin),
            split_heads(sv, (SWA_KV_HEADS, HEAD_DIM)), ks_c, vs_c, sink)
        o_d = pool_mixer(du, d_w[l], d_scale[l])
        x = x + g_mix * merge_branches((o_a, o_b, o_s, o_d), gl, w_branch[l], w_out[l])

        if not last:
            oc_a = ctx_attention(rmsnorm(split_heads(aqc, (NA_HEADS, 1, HEAD_DIM)), a_q_norm[l]), ka_c, va_c)
            oc_b = diff_output(diff_attention_ctx(
                rmsnorm(split_heads(bqc, (DIFF_HEADS, 2, HEAD_DIM)), b_q_norm[l]), kb_c, vb_c, lam), b_subln[l], lam_init)
            oc_s = ctx_attention(rmsnorm(split_heads(sqc, (SWA_KV_HEADS, SWA_GROUP, HEAD_DIM)), c_q_norm[l]), ks_c, vs_c, sink)
            oc_d = pool_mixer(duc, d_w[l], d_scale[l])
            ctx = ctx + mod_c[2] * merge_branches((oc_a, oc_b, oc_s, oc_d), glc, w_branch[l], w_out[l])
            ctx = ctx + mod_c[5] * swiglu(modulate(rmsnorm(ctx, norm_ffn[l]), mod_c[3], mod_c[4]),
                                          w_ffn_gate[l], w_ffn_up[l], w_ffn_down[l])

        x = x + g_ffn * swiglu(modulate(rmsnorm(x, norm_ffn[l]), sh_ffn, sc_ffn),
                               w_ffn_gate[l], w_ffn_up[l], w_ffn_down[l])
    return x
```

```python
import functools
import math

import jax
import jax.numpy as jnp
from jax import lax
from jax.experimental import pallas as pl
from jax.experimental.pallas import tpu as pltpu

F32 = jnp.float32
BF16 = jnp.bfloat16

HEAD_DIM = 64
LANES = 128
GRID_W = 64
N_BRANCH = 4
NA_ROWS = 8
NA_COLS = 16
SWA_WINDOW = 128
SWA_KV_HEADS = 2
POOL_WINDOWS = (2, 4, 8, 16)
ROPE_BASE = 10000.0
NORM_EPS = 1e-6
MASK_VALUE = -1e30
ATTN_SCALE = HEAD_DIM ** -0.5
MIX_W = 512
VMEM_LIMIT = 56 * 1024 * 1024


def _cparams(sem):
    return pltpu.CompilerParams(dimension_semantics=sem, vmem_limit_bytes=VMEM_LIMIT)


def _dot(a, b):
    return jnp.dot(a, b, preferred_element_type=F32)


def _dot_nt(a, b):
    return lax.dot_general(a, b, (((1,), (1,)), ((), ())), preferred_element_type=F32)


def _split_dot(a, b_bf16):
    hi = a.astype(BF16)
    lo = (a - hi.astype(F32)).astype(BF16)
    return _dot(hi, b_bf16) + _dot(lo, b_bf16)


def _lane_lo(shape):
    return lax.broadcasted_iota(jnp.int32, shape, len(shape) - 1) % LANES < HEAD_DIM


def _stack_halves(q2):
    lo = _lane_lo(q2.shape)
    zero = jnp.zeros_like(q2)
    return jnp.concatenate([jnp.where(lo, q2, zero), jnp.where(lo, zero, q2)], axis=0)


def _merge_halves(o, t):
    lo = _lane_lo((t, LANES))
    return jnp.where(lo, o[:t], o[t:])


def _mod_kernel(c_ref, w_ref, b_ref, o_ref):
    c = c_ref[...]
    act = (c * jax.nn.sigmoid(c)).astype(BF16)
    o_ref[...] = _dot(act, w_ref[...].astype(BF16)) + b_ref[...]


def _modulation(cc, w_ada, b_ada):
    L, D, N = w_ada.shape
    R = cc.shape[0]
    tn = 1024
    return pl.pallas_call(
        _mod_kernel,
        grid=(L, N // tn),
        in_specs=[
            pl.BlockSpec((R, D), lambda l, j: (0, 0)),
            pl.BlockSpec((None, D, tn), lambda l, j: (l, 0, j)),
            pl.BlockSpec((None, 1, tn), lambda l, j: (l, 0, j)),
        ],
        out_specs=pl.BlockSpec((None, R, tn), lambda l, j: (l, 0, j)),
        out_shape=jax.ShapeDtypeStruct((L, R, N), F32),
        compiler_params=_cparams(("parallel", "parallel")),
        name="adaln_mod",
    )(cc, w_ada, b_ada.reshape(L, 1, N))


def _modnorm_into(dst_ref, x_ref, g_ref, sh_ref, sc_ref, chunk=128):
    tm = x_ref.shape[0]
    g = g_ref[...]
    sh = sh_ref[...]
    sc1 = 1.0 + sc_ref[...]

    def body(r, carry):
        rows = pl.ds(pl.multiple_of(r * chunk, chunk), chunk)
        xf = x_ref[rows, :]
        ms = jnp.mean(xf * xf, axis=-1, keepdims=True)
        y = xf * lax.rsqrt(ms + NORM_EPS) * g
        dst_ref[rows, :] = (y * sc1 + sh).astype(dst_ref.dtype)
        return carry

    lax.fori_loop(0, tm // chunk, body, 0)


def _headnorm(y, bd_ref, gvec):
    ss = _split_dot(y * y, bd_ref[...])
    return y * lax.rsqrt(ss * (1.0 / HEAD_DIM) + NORM_EPS) * gvec


def _rope(n, cos, sin_signed):
    w = n.shape[-1]
    lane = lax.broadcasted_iota(jnp.int32, n.shape, 1)
    first = (lane % 32) < 16
    partner = jnp.where(first, pltpu.roll(n, w - 16, 1), pltpu.roll(n, 16, 1))
    return n * cos + partner * sin_signed


def _proj_kernel(x_ref, g_ref, sh_ref, sc_ref, w_ref, gv_ref, cos_ref, sin_ref, bd_ref,
                 qkv_ref, du_ref, h_ref):
    j = pl.program_id(1)

    @pl.when(j == 0)
    def _():
        _modnorm_into(h_ref, x_ref, g_ref, sh_ref, sc_ref)

    y = _dot(h_ref[...], w_ref[...])
    gv = gv_ref[...]

    @pl.when(j == 0)
    def _():
        du_ref[...] = y

    @pl.when((j == 1) | (j == 2))
    def _():
        qkv_ref[...] = _headnorm(y, bd_ref, gv).astype(BF16)

    @pl.when((j == 4) | (j == 5) | (j == 7))
    def _():
        n = _headnorm(y, bd_ref, gv)
        qkv_ref[...] = _rope(n, cos_ref[...], sin_ref[...]).astype(BF16)

    @pl.when((j == 3) | (j == 6))
    def _():
        qkv_ref[...] = y.astype(BF16)

    @pl.when(j == 8)
    def _():
        half = y.shape[-1] // 2
        n = _headnorm(y, bd_ref, gv)[:, :half]
        kk = _rope(n, cos_ref[:, :half], sin_ref[:, :half])
        qkv_ref[:, :half] = kk.astype(BF16)
        qkv_ref[:, half:] = y[:, half:].astype(BF16)


def _project(x2, mod4, row_of_tile, g_norm, w_cat, gvecs, cos_t, sin_t, table_block, bd, tm):
    M, D = x2.shape
    nblk = w_cat.shape[1] // MIX_W
    nq = nblk - 1
    mod_spec = lambda k: pl.BlockSpec((None, None, 1, D), lambda i, j: (row_of_tile(i), k, 0, 0))
    return pl.pallas_call(
        _proj_kernel,
        grid=(M // tm, nblk),
        in_specs=[
            pl.BlockSpec((tm, D), lambda i, j: (i, 0)),
            pl.BlockSpec((1, D), lambda i, j: (0, 0)),
            mod_spec(0), mod_spec(1),
            pl.BlockSpec((D, MIX_W), lambda i, j: (0, j)),
            pl.BlockSpec((None, 1, MIX_W), lambda i, j: (j, 0, 0)),
            pl.BlockSpec((tm, MIX_W), lambda i, j: (table_block(i), 0)),
            pl.BlockSpec((tm, MIX_W), lambda i, j: (table_block(i), 0)),
            pl.BlockSpec((MIX_W, MIX_W), lambda i, j: (0, 0)),
        ],
        out_specs=[
            pl.BlockSpec((tm, MIX_W), lambda i, j: (i, jnp.maximum(j - 1, 0))),
            pl.BlockSpec((tm, MIX_W), lambda i, j: (i, 0)),
            pl.BlockSpec((tm, D), lambda i, j: (i, 0)),
        ],
        out_shape=[
            jax.ShapeDtypeStruct((M, nq * MIX_W), BF16),
            jax.ShapeDtypeStruct((M, MIX_W), F32),
            jax.ShapeDtypeStruct((M, D), BF16),
        ],
        compiler_params=_cparams(("parallel", "arbitrary")),
        name="in_proj",
    )(x2, g_norm, mod4, mod4, w_cat, gvecs, cos_t, sin_t, bd)


_QA, _KA, _VA = 0, 4, 8
_QB, _KB, _VB = 12, 16, 20
_QC = 24
_KC, _VC = 28, 30


def _softmax_pv(parts, extra_logit=None):
    m = parts[0][0].max(axis=-1, keepdims=True)
    for s, _ in parts[1:]:
        m = jnp.maximum(m, s.max(axis=-1, keepdims=True))
    if extra_logit is not None:
        m = jnp.maximum(m, extra_logit)
    l = None
    o = None
    for s, v in parts:
        p = jnp.exp(s - m)
        ls = p.sum(axis=-1, keepdims=True)
        os_ = _dot(p.astype(BF16), v)
        l = ls if l is None else l + ls
        o = os_ if o is None else o + os_
    if extra_logit is not None:
        l = l + jnp.exp(extra_logit - m)
    return o / l


def _na_kernel(q_ref, k_ref, v_ref, kc_ref, vc_ref, bias_ref, o_ref, *, rows, rb):
    r0 = pl.program_id(2) * rb
    kc = kc_ref[...]
    vc = vc_ref[...]

    def body(rr, carry):
        r = r0 + rr
        start = jnp.clip(r - NA_ROWS // 2, 0, rows - NA_ROWS)
        off = start - r + (NA_ROWS - 1)
        qrows = pl.ds(pl.multiple_of(rr * GRID_W, GRID_W), GRID_W)
        krows = pl.ds(pl.multiple_of(start * GRID_W, GRID_W), NA_ROWS * GRID_W)
        qs = _stack_halves(q_ref[qrows, :])
        s = _dot_nt(qs, k_ref[krows, :]) + bias_ref[off]
        sc = _dot_nt(qs, kc)
        o = _softmax_pv([(s, v_ref[krows, :]), (sc, vc)])
        o_ref[qrows, :] = _merge_halves(o, GRID_W).astype(o_ref.dtype)
        return carry

    lax.fori_loop(0, rb, body, 0)


def _na_attention(qkv, qkv_c, bias, B, S, Lc):
    rows = S // GRID_W
    rb = 4
    nrb = rows // rb
    tq = rb * GRID_W
    npair = MIX_W // LANES
    return pl.pallas_call(
        functools.partial(_na_kernel, rows=rows, rb=rb),
        grid=(B, npair, nrb),
        in_specs=[
            pl.BlockSpec((tq, LANES), lambda b, p, r: (b * nrb + r, _QA + p)),
            pl.BlockSpec((S, LANES), lambda b, p, r: (b, _KA + p)),
            pl.BlockSpec((S, LANES), lambda b, p, r: (b, _VA + p)),
            pl.BlockSpec((Lc, LANES), lambda b, p, r: (b, _KA + p)),
            pl.BlockSpec((Lc, LANES), lambda b, p, r: (b, _VA + p)),
            pl.BlockSpec((None, NA_ROWS, 2 * GRID_W, NA_ROWS * GRID_W), lambda b, p, r: (p, 0, 0, 0)),
        ],
        out_specs=pl.BlockSpec((tq, LANES), lambda b, p, r: (b * nrb + r, p)),
        out_shape=jax.ShapeDtypeStruct((B * S, MIX_W), BF16),
        compiler_params=_cparams(("parallel", "parallel", "arbitrary")),
        name="na_attention",
    )(qkv, qkv, qkv, qkv_c, qkv_c, bias)


def _na_bias_table(rpb):
    H = rpb.shape[0]
    col = jnp.arange(GRID_W)
    col_start = jnp.clip(col - NA_COLS // 2, 0, GRID_W - NA_COLS)
    col_mask = (col[None, :] >= col_start[:, None]) & (col[None, :] < col_start[:, None] + NA_COLS)
    dc = jnp.clip(col[None, :] - col[:, None], -(NA_COLS - 1), NA_COLS - 1) + (NA_COLS - 1)
    dr = jnp.arange(NA_ROWS)[:, None] + jnp.arange(NA_ROWS)[None, :]
    t = rpb[:, dr[:, :, None, None], dc[None, None, :, :]].astype(F32)
    t = jnp.where(col_mask[None, None, None], t, MASK_VALUE)
    t = t.transpose(0, 1, 3, 2, 4).reshape(H // 2, 2, NA_ROWS, GRID_W, NA_ROWS * GRID_W)
    return t.transpose(0, 2, 1, 3, 4).reshape(H // 2, NA_ROWS, 2 * GRID_W, NA_ROWS * GRID_W)


def _lambda_full(lamp_ref, lam_init):
    lp = lamp_ref[...]
    a = jnp.sum(lp[0:1] * lp[1:2], axis=-1, keepdims=True)
    b = jnp.sum(lp[2:3] * lp[3:4], axis=-1, keepdims=True)
    return jnp.exp(a) - jnp.exp(b) + lam_init


def _subln(o, g, lam_init):
    ms = jnp.mean(o * o, axis=-1, keepdims=True)
    return o * lax.rsqrt(ms + NORM_EPS) * g * (1.0 - lam_init)


def _diff_kernel(q_ref, k_ref, v_ref, kc_ref, vc_ref, lamp_ref, g_ref, o_ref, *, lam_init, tk):
    tq = q_ref.shape[0]
    S = k_ref.shape[0]
    qs = _stack_halves(q_ref[...])

    def step(s, v, m, l, acc):
        m_new = jnp.maximum(m, s.max(axis=-1, keepdims=True))
        alpha = jnp.exp(m - m_new)
        p = jnp.exp(s - m_new)
        l = alpha * l + p.sum(axis=-1, keepdims=True)
        acc = alpha * acc + _dot(p.astype(BF16), v)
        return m_new, l, acc

    def body(kb, carry):
        rows = pl.ds(pl.multiple_of(kb * tk, tk), tk)
        return step(_dot_nt(qs, k_ref[rows, :]), v_ref[rows, :], *carry)

    init = (jnp.full((2 * tq, 1), MASK_VALUE, F32), jnp.zeros((2 * tq, 1), F32),
            jnp.zeros((2 * tq, LANES), F32))
    carry = lax.fori_loop(0, S // tk, body, init)
    m, l, acc = step(_dot_nt(qs, kc_ref[...]), vc_ref[...], *carry)
    on = acc / l
    lam = _lambda_full(lamp_ref, lam_init)
    o = on[:tq] - lam * on[tq:]
    o_ref[...] = _subln(o, g_ref[...], lam_init).astype(o_ref.dtype)


def _diff_attention(qkv, qkv_c, lamp, subln_g, lam_init, B, S, Lc):
    tq = 256
    nq = S // tq
    nh = MIX_W // LANES
    return pl.pallas_call(
        functools.partial(_diff_kernel, lam_init=lam_init, tk=512),
        grid=(B, nh, nq),
        in_specs=[
            pl.BlockSpec((tq, LANES), lambda b, h, i: (b * nq + i, _QB + h)),
            pl.BlockSpec((S, LANES), lambda b, h, i: (b, _KB + h)),
            pl.BlockSpec((S, LANES), lambda b, h, i: (b, _VB + h)),
            pl.BlockSpec((Lc, LANES), lambda b, h, i: (b, _KB + h)),
            pl.BlockSpec((Lc, LANES), lambda b, h, i: (b, _VB + h)),
            pl.BlockSpec((4, HEAD_DIM), lambda b, h, i: (0, 0)),
            pl.BlockSpec((1, LANES), lambda b, h, i: (0, 0)),
        ],
        out_specs=pl.BlockSpec((tq, LANES), lambda b, h, i: (b * nq + i, h)),
        out_shape=jax.ShapeDtypeStruct((B * S, MIX_W), BF16),
        compiler_params=_cparams(("parallel", "parallel", "arbitrary")),
        name="diff_attention",
    )(qkv, qkv, qkv, qkv_c, qkv_c, lamp, subln_g)


def _stack_group(q4):
    return jnp.concatenate([_stack_halves(q4[:, :LANES]), _stack_halves(q4[:, LANES:])], axis=0)


def _merge_group(o, t):
    return jnp.concatenate([_merge_halves(o[:2 * t], t), _merge_halves(o[2 * t:], t)], axis=-1)


def _sink_column(sink_ref, c, t):
    group = sink_ref.shape[0] // SWA_KV_HEADS
    return jnp.concatenate([jnp.full((t, 1), sink_ref[c * group + g], F32) for g in range(group)], axis=0)


def _swa_kernel(sink_ref, q_ref, k_ref, v_ref, kc_ref, vc_ref, o_ref):
    c = pl.program_id(1)
    n = pl.program_id(2)
    W = SWA_WINDOW
    S = k_ref.shape[0]
    ks = pl.multiple_of(jnp.clip((n - 1) * W, 0, S - 3 * W), W)
    krows = pl.ds(ks, 3 * W)
    qs = _stack_group(q_ref[...])
    s = _dot_nt(qs, k_ref[krows, :])
    qpos = n * W + lax.broadcasted_iota(jnp.int32, s.shape, 0) % W
    kpos = ks + lax.broadcasted_iota(jnp.int32, s.shape, 1)
    s = jnp.where(jnp.abs(kpos - qpos) <= W, s, MASK_VALUE)
    sc = _dot_nt(qs, kc_ref[...])
    o = _softmax_pv([(s, v_ref[krows, :]), (sc, vc_ref[...])], extra_logit=_sink_column(sink_ref, c, W))
    o_ref[...] = _merge_group(o, W).astype(o_ref.dtype)


def _swa_attention(qkv, qkv_c, sink, B, S, Lc):
    W = SWA_WINDOW
    nb = S // W
    return pl.pallas_call(
        _swa_kernel,
        grid=(B, SWA_KV_HEADS, nb),
        in_specs=[
            pl.BlockSpec(memory_space=pltpu.SMEM),
            pl.BlockSpec((W, 2 * LANES), lambda b, c, n: (b * nb + n, _QC // 2 + c)),
            pl.BlockSpec((S, LANES), lambda b, c, n: (b, _KC + c)),
            pl.BlockSpec((S, LANES), lambda b, c, n: (b, _VC + c)),
            pl.BlockSpec((Lc, LANES), lambda b, c, n: (b, _KC + c)),
            pl.BlockSpec((Lc, LANES), lambda b, c, n: (b, _VC + c)),
        ],
        out_specs=pl.BlockSpec((W, 2 * LANES), lambda b, c, n: (b * nb + n, c)),
        out_shape=jax.ShapeDtypeStruct((B * S, MIX_W), BF16),
        compiler_params=_cparams(("parallel", "parallel", "arbitrary")),
        name="swa_attention",
    )(sink, qkv, qkv, qkv, qkv_c, qkv_c)


def _ctx_attn_kernel(sink_ref, qkv_ref, lamp_ref, g_ref, oa_ref, ob_ref, os_ref, *, lam_init):
    Lc = qkv_ref.shape[0]

    def cols(blk, width=LANES):
        return qkv_ref[:, blk * LANES: blk * LANES + width]

    for p in range(MIX_W // LANES):
        qs = _stack_halves(cols(_QA + p))
        o = _softmax_pv([(_dot_nt(qs, cols(_KA + p)), cols(_VA + p))])
        oa_ref[:, p * LANES:(p + 1) * LANES] = _merge_halves(o, Lc).astype(oa_ref.dtype)

    lam = _lambda_full(lamp_ref, lam_init)
    for h in range(MIX_W // LANES):
        qs = _stack_halves(cols(_QB + h))
        on = _softmax_pv([(_dot_nt(qs, cols(_KB + h)), cols(_VB + h))])
        o = on[:Lc] - lam * on[Lc:]
        ob_ref[:, h * LANES:(h + 1) * LANES] = _subln(o, g_ref[...], lam_init).astype(ob_ref.dtype)

    for c in range(SWA_KV_HEADS):
        qs = _stack_group(cols(_QC + 2 * c, 2 * LANES))
        o = _softmax_pv([(_dot_nt(qs, cols(_KC + c)), cols(_VC + c))],
                        extra_logit=_sink_column(sink_ref, c, Lc))
        os_ref[:, c * 2 * LANES:(c + 1) * 2 * LANES] = _merge_group(o, Lc).astype(os_ref.dtype)


def _ctx_attention(qkv_c, sink, lamp, subln_g, lam_init, B, Lc):
    NQ = qkv_c.shape[1]
    out = jax.ShapeDtypeStruct((B * Lc, MIX_W), BF16)
    ospec = pl.BlockSpec((Lc, MIX_W), lambda b: (b, 0))
    return pl.pallas_call(
        functools.partial(_ctx_attn_kernel, lam_init=lam_init),
        grid=(B,),
        in_specs=[
            pl.BlockSpec(memory_space=pltpu.SMEM),
            pl.BlockSpec((Lc, NQ), lambda b: (b, 0)),
            pl.BlockSpec((4, HEAD_DIM), lambda b: (0, 0)),
            pl.BlockSpec((1, LANES), lambda b: (0, 0)),
        ],
        out_specs=[ospec, ospec, ospec],
        out_shape=[out, out, out],
        compiler_params=_cparams(("parallel",)),
        name="ctx_attention",
    )(sink, qkv_c, lamp, subln_g)


def _shift_down(x, k, t_idx):
    return jnp.where(t_idx >= k, pltpu.roll(x, k, 0), 0.0)


def _shift_up(x, k, t_idx):
    T = x.shape[0]
    return jnp.where(t_idx < T - k, pltpu.roll(x, T - k, 0), 0.0)


def _pool_kernel(u_ref, w_ref, scale_ref, o_ref):
    T = u_ref.shape[0]
    pg = u_ref.shape[1] // len(POOL_WINDOWS)
    t_idx = lax.broadcasted_iota(jnp.int32, (T, pg), 0)
    for g, w in enumerate(POOL_WINDOWS):
        u = u_ref[:, g * pg:(g + 1) * pg]
        back = u
        fwd = u
        span = 1
        while span < w // 2:
            back = back + _shift_down(back, span, t_idx)
            fwd = fwd + _shift_up(fwd, span, t_idx)
            span *= 2
        win = _shift_down(back, 1, t_idx) + fwd
        lo = jnp.clip(t_idx - w // 2, 0, T)
        hi = jnp.clip(t_idx - w // 2 + w, 0, T)
        d = (win / (hi - lo).astype(F32) - u).astype(BF16)
        mixed = _dot(d, w_ref[g])
        o_ref[:, g * pg:(g + 1) * pg] = (mixed * scale_ref[:, g * pg:(g + 1) * pg]).astype(o_ref.dtype)


def _pool_mixer(du, w_grp, scale, nseq, T):
    return pl.pallas_call(
        _pool_kernel,
        grid=(nseq,),
        in_specs=[
            pl.BlockSpec((T, MIX_W), lambda b: (b, 0)),
            pl.BlockSpec(w_grp.shape, lambda b: (0, 0, 0)),
            pl.BlockSpec((1, MIX_W), lambda b: (0, 0)),
        ],
        out_specs=pl.BlockSpec((T, MIX_W), lambda b: (b, 0)),
        out_shape=jax.ShapeDtypeStruct((nseq * T, MIX_W), BF16),
        compiler_params=_cparams(("parallel",)),
        name="pool_mixer",
    )(du, w_grp, scale)


def _merge_kernel(h_ref, oa_ref, ob_ref, os_ref, od_ref, wg_ref, wb_ref, acc_ref, f32_ref):
    tn = acc_ref.shape[1]
    h = h_ref[...]
    for n, o_ref in enumerate((oa_ref, ob_ref, os_ref, od_ref)):
        gate = jax.nn.sigmoid(_dot(h, wg_ref[:, n * tn:(n + 1) * tn]))
        term = gate * _dot(o_ref[...], wb_ref[n])
        if n == 0:
            f32_ref[...] = term
        else:
            f32_ref[...] += term
    acc_ref[...] = f32_ref[...].astype(acc_ref.dtype)


def _merge(h, outs, w_gate_r, w_br, tm, tn):
    M, D = h.shape
    ospec = pl.BlockSpec((tm, MIX_W), lambda i, j: (i, 0))
    return pl.pallas_call(
        _merge_kernel,
        grid=(M // tm, D // tn),
        in_specs=[
            pl.BlockSpec((tm, D), lambda i, j: (i, 0)),
            ospec, ospec, ospec, ospec,
            pl.BlockSpec((D, N_BRANCH * tn), lambda i, j: (0, j)),
            pl.BlockSpec((N_BRANCH, MIX_W, tn), lambda i, j: (0, 0, j)),
        ],
        out_specs=pl.BlockSpec((tm, tn), lambda i, j: (i, j)),
        out_shape=jax.ShapeDtypeStruct((M, D), BF16),
        scratch_shapes=[pltpu.VMEM((tm, tn), F32)],
        compiler_params=_cparams(("parallel", "arbitrary")),
        name="gated_merge",
    )(h, *outs, w_gate_r, w_br)


def _outproj_kernel(a_ref, w_ref, x_ref, g_ref, o_ref):
    o_ref[...] = x_ref[...] + g_ref[...] * _dot(a_ref[...], w_ref[...])


def _out_project(acc, w_o, x2, mod4, row_of_tile, tm, tn):
    M, D = x2.shape
    return pl.pallas_call(
        _outproj_kernel,
        grid=(M // tm, D // tn),
        in_specs=[
            pl.BlockSpec((tm, D), lambda i, j: (i, 0)),
            pl.BlockSpec((D, tn), lambda i, j: (0, j)),
            pl.BlockSpec((tm, tn), lambda i, j: (i, j)),
            pl.BlockSpec((None, None, 1, tn), lambda i, j: (row_of_tile(i), 2, 0, j)),
        ],
        out_specs=pl.BlockSpec((tm, tn), lambda i, j: (i, j)),
        out_shape=jax.ShapeDtypeStruct((M, D), F32),
        compiler_params=_cparams(("parallel", "arbitrary")),
        name="out_proj",
    )(acc, w_o, x2, mod4)


def _ffn_kernel(x_ref, g_ref, sh_ref, sc_ref, gate_ref, wg_ref, wu_ref, wd_ref, o_ref, h_ref):
    f = pl.program_id(1)

    @pl.when(f == 0)
    def _():
        _modnorm_into(h_ref, x_ref, g_ref, sh_ref, sc_ref)

    h = h_ref[...]
    a = _dot(h, wg_ref[...])
    u = _dot(h, wu_ref[...])
    part = _dot((a * jax.nn.sigmoid(a) * u).astype(BF16), wd_ref[...])

    @pl.when(f == 0)
    def _():
        o_ref[...] = part

    @pl.when(f > 0)
    def _():
        o_ref[...] += part

    @pl.when(f == pl.num_programs(1) - 1)
    def _():
        o_ref[...] = x_ref[...] + gate_ref[...] * o_ref[...]


def _ffn(x2, mod4, row_of_tile, g_norm, wg, wu, wd, tm, tf):
    M, D = x2.shape
    F = wg.shape[1]
    mod_spec = lambda k: pl.BlockSpec((None, None, 1, D), lambda i, f: (row_of_tile(i), k, 0, 0))
    return pl.pallas_call(
        _ffn_kernel,
        grid=(M // tm, F // tf),
        in_specs=[
            pl.BlockSpec((tm, D), lambda i, f: (i, 0)),
            pl.BlockSpec((1, D), lambda i, f: (0, 0)),
            mod_spec(3), mod_spec(4), mod_spec(5),
            pl.BlockSpec((D, tf), lambda i, f: (0, f)),
            pl.BlockSpec((D, tf), lambda i, f: (0, f)),
            pl.BlockSpec((tf, D), lambda i, f: (f, 0)),
        ],
        out_specs=pl.BlockSpec((tm, D), lambda i, f: (i, 0)),
        out_shape=jax.ShapeDtypeStruct((M, D), F32),
        scratch_shapes=[pltpu.VMEM((tm, D), BF16)],
        compiler_params=_cparams(("parallel", "arbitrary")),
        name="swiglu_ffn",
    )(x2, g_norm, mod4, mod4, mod4, wg, wu, wd)


def _rope_tables(S):
    t = jnp.arange(S)
    pos = jnp.stack([t // GRID_W, t % GRID_W], axis=-1).astype(F32)
    n_freq = HEAD_DIM // 4
    inv = ROPE_BASE ** (-jnp.arange(n_freq, dtype=F32) / n_freq)
    ang = pos[:, :, None] * inv
    cos, sin = jnp.cos(ang), jnp.sin(ang)
    cos_h = jnp.concatenate([cos, cos], axis=-1).reshape(S, HEAD_DIM)
    sin_h = jnp.concatenate([-sin, sin], axis=-1).reshape(S, HEAD_DIM)
    reps = MIX_W // HEAD_DIM
    return jnp.tile(cos_h, (1, reps)), jnp.tile(sin_h, (1, reps))


def _split_w_in(w_in_l):
    sizes = (MIX_W,) * 7 + (SWA_KV_HEADS * HEAD_DIM,) * 2 + (MIX_W,)
    offs = [0]
    for s in sizes:
        offs.append(offs[-1] + s)
    parts = [w_in_l[:, offs[i]:offs[i + 1]] for i in range(len(sizes))]
    return parts, w_in_l[:, offs[-1]:]


def _dup_heads(w):
    D = w.shape[0]
    return jnp.repeat(w.reshape(D, SWA_KV_HEADS, 1, HEAD_DIM), 2, axis=2).reshape(D, 2 * SWA_KV_HEADS * HEAD_DIM)


def _tile_heads(g, scale=1.0):
    return jnp.tile(g.astype(F32) * scale, MIX_W // HEAD_DIM)


def _pick_tile(M, unit, target):
    t = min(target, M)
    while M % t or (t > unit and t % unit) or (t < unit and unit % t):
        t //= 2
    return t


def kernel(x, c, ctx, c_ctx, w_ada, b_ada, norm_mix, norm_ffn, w_in, a_q_norm, a_k_norm, a_rpb, b_q_norm, b_k_norm, b_lam_q1, b_lam_k1, b_lam_q2, b_lam_k2, b_subln, c_q_norm, c_k_norm, c_sink, d_w, d_scale, w_branch, w_out, w_ffn_gate, w_ffn_up, w_ffn_down):
    B, S, D = x.shape
    Lc = ctx.shape[1]
    L = w_ada.shape[0]
    assert D == N_BRANCH * MIX_W and S % GRID_W == 0 and S // GRID_W >= NA_ROWS and S >= 3 * SWA_WINDOW

    tm = _pick_tile(S, S, 512)
    tmc = _pick_tile(B * Lc, 8, 512)
    tiles_per_seq = S // tm
    lat_row = lambda i: i // tiles_per_seq
    ctx_row = lambda i: B
    n_mod_rows = -(-(B + 1) // 8) * 8

    cc = jnp.zeros((n_mod_rows, D), F32).at[:B].set(c).at[B].set(c_ctx)
    mod = _modulation(cc, w_ada, b_ada).reshape(L, n_mod_rows, 6, 1, D)

    cos_t, sin_t = _rope_tables(S)
    cos_id = jnp.ones((tmc, MIX_W), F32)
    sin_id = jnp.zeros((tmc, MIX_W), F32)
    lane = jnp.arange(MIX_W)
    bd = (lane[:, None] // HEAD_DIM == lane[None, :] // HEAD_DIM).astype(BF16)

    x2 = x.reshape(B * S, D)
    c2 = ctx.reshape(B * Lc, D)
    ones = jnp.ones((MIX_W,), F32)
    tn_merge = 256

    for l in range(L):
        last = l == L - 1
        lam_init = 0.8 - 0.6 * math.exp(-0.3 * l)
        mod_l = mod[l]

        (aq, ak, av, bq, bk, bv, sq, sk, sv, du), w_gl = _split_w_in(w_in[l])
        w_cat = jnp.concatenate([du, aq, ak, av, bq, bk, bv, sq, _dup_heads(sk), _dup_heads(sv)], axis=1).astype(BF16)
        half = MIX_W // 2
        gvecs = jnp.stack([
            ones,
            _tile_heads(a_q_norm[l], ATTN_SCALE), _tile_heads(a_k_norm[l]), ones,
            _tile_heads(b_q_norm[l], ATTN_SCALE), _tile_heads(b_k_norm[l]), ones,
            _tile_heads(c_q_norm[l], ATTN_SCALE),
            jnp.concatenate([_tile_heads(c_k_norm[l])[:half], ones[half:]]),
        ]).reshape(9, 1, MIX_W)
        w_gate_r = (w_gl.reshape(D, N_BRANCH, D // tn_merge, tn_merge).transpose(0, 2, 1, 3)
                    .reshape(D, N_BRANCH * D).astype(BF16))
        w_br = w_branch[l].astype(BF16)
        w_o = w_out[l].astype(BF16)
        wg, wu, wd = w_ffn_gate[l].astype(BF16), w_ffn_up[l].astype(BF16), w_ffn_down[l].astype(BF16)
        g_mix_norm = norm_mix[l].reshape(1, D)
        g_ffn_norm = norm_ffn[l].reshape(1, D)
        lamp = jnp.stack([b_lam_q1[l], b_lam_k1[l], b_lam_q2[l], b_lam_k2[l]]).astype(F32)
        subln_g = b_subln[l].reshape(1, LANES).astype(F32)
        sink = c_sink[l].astype(F32)
        w_pool = d_w[l].astype(BF16)
        pool_scale = d_scale[l].reshape(1, MIX_W).astype(F32)

        qkv, du_l, h_l = _project(x2, mod_l, lat_row, g_mix_norm, w_cat, gvecs, cos_t, sin_t,
                                  lambda i: i % tiles_per_seq, bd, tm)
        qkv_c, du_c, h_c = _project(c2, mod_l, ctx_row, g_mix_norm, w_cat, gvecs, cos_id, sin_id,
                                    lambda i: 0, bd, tmc)

        o_a = _na_attention(qkv, qkv_c, _na_bias_table(a_rpb[l]), B, S, Lc)
        o_b = _diff_attention(qkv, qkv_c, lamp, subln_g, lam_init, B, S, Lc)
        o_s = _swa_attention(qkv, qkv_c, sink, B, S, Lc)
        o_d = _pool_mixer(du_l, w_pool, pool_scale, B, S)
        acc = _merge(h_l, (o_a, o_b, o_s, o_d), w_gate_r, w_br, tm, tn_merge)
        x2 = _out_project(acc, w_o, x2, mod_l, lat_row, tm, 512)

        if not last:
            oc_a, oc_b, oc_s = _ctx_attention(qkv_c, sink, lamp, subln_g, lam_init, B, Lc)
            oc_d = _pool_mixer(du_c, w_pool, pool_scale, B, Lc)
            acc_c = _merge(h_c, (oc_a, oc_b, oc_s, oc_d), w_gate_r, w_br, tmc, tn_merge)
            c2 = _out_project(acc_c, w_o, c2, mod_l, ctx_row, tmc, 512)
            c2 = _ffn(c2, mod_l, ctx_row, g_ffn_norm, wg, wu, wd, tmc, 512)

        x2 = _ffn(x2, mod_l, lat_row, g_ffn_norm, wg, wu, wd, tm, 512)

    return x2.reshape(B, S, D)
```

```python
import functools
import math

import numpy as np
import jax
import jax.numpy as jnp
from jax import lax
from jax.experimental import pallas as pl
from jax.experimental.pallas import tpu as pltpu

F32 = jnp.float32
BF16 = jnp.bfloat16

HEAD_DIM = 64
LANES = 128
GRID_W = 64
N_BRANCH = 4
NA_ROWS = 8
NA_COLS = 16
SWA_WINDOW = 128
SWA_KV_HEADS = 2
POOL_WINDOWS = (2, 4, 8, 16)
ROPE_BASE = 10000.0
NORM_EPS = 1e-6
MASK_VALUE = -1e30
ATTN_SCALE = HEAD_DIM ** -0.5
MIX_W = 512
VMEM_LIMIT = 56 * 1024 * 1024


def _cparams(sem):
    return pltpu.CompilerParams(dimension_semantics=sem, vmem_limit_bytes=VMEM_LIMIT)


def _dot(a, b):
    return jnp.dot(a, b, preferred_element_type=F32)


def _dot_nt(a, b):
    return lax.dot_general(a, b, (((1,), (1,)), ((), ())), preferred_element_type=F32)


def _lane_lo(shape):
    return lax.broadcasted_iota(jnp.int32, shape, len(shape) - 1) % LANES < HEAD_DIM


def _stack_halves(q2):
    lo = _lane_lo(q2.shape)
    zero = jnp.zeros_like(q2)
    return jnp.concatenate([jnp.where(lo, q2, zero), jnp.where(lo, zero, q2)], axis=0)


def _merge_halves(o, t):
    lo = _lane_lo((t, LANES))
    return jnp.where(lo, o[:t], o[t:])


def _mod_kernel(c_ref, w_ref, b_ref, o_ref):
    c = c_ref[...]
    act = (c * jax.nn.sigmoid(c)).astype(BF16)
    o_ref[...] = _dot(act, w_ref[...].astype(BF16)) + b_ref[...]


def _modulation(cc, w_ada, b_ada):
    L, D, N = w_ada.shape
    R = cc.shape[0]
    tn = 1024
    return pl.pallas_call(
        _mod_kernel,
        grid=(L, N // tn),
        in_specs=[
            pl.BlockSpec((R, D), lambda l, j: (0, 0)),
            pl.BlockSpec((None, D, tn), lambda l, j: (l, 0, j)),
            pl.BlockSpec((None, 1, tn), lambda l, j: (l, 0, j)),
        ],
        out_specs=pl.BlockSpec((None, R, tn), lambda l, j: (l, 0, j)),
        out_shape=jax.ShapeDtypeStruct((L, R, N), F32),
        compiler_params=_cparams(("parallel", "parallel")),
        name="adaln_mod",
    )(cc, w_ada, b_ada.reshape(L, 1, N))


def _modnorm_into(dst_ref, x_ref, g_ref, sh_ref, sc_ref, chunk=128):
    tm = x_ref.shape[0]
    g = g_ref[...]
    sh = sh_ref[...]
    sc1 = 1.0 + sc_ref[...]

    def body(r, carry):
        rows = pl.ds(pl.multiple_of(r * chunk, chunk), chunk)
        xf = x_ref[rows, :]
        ms = jnp.mean(xf * xf, axis=-1, keepdims=True)
        y = xf * lax.rsqrt(ms + NORM_EPS) * g
        dst_ref[rows, :] = (y * sc1 + sh).astype(dst_ref.dtype)
        return carry

    lax.fori_loop(0, tm // chunk, body, 0)


def _proj_kernel(x_ref, g_ref, sh_ref, sc_ref, w_ref, pv_ref, cos_ref, sin_ref, bd_ref,
                 qkv_ref, du_ref, h_ref, y_ref):
    j = pl.program_id(1)

    @pl.when(j == 0)
    def _():
        _modnorm_into(h_ref, x_ref, g_ref, sh_ref, sc_ref)
        y_ref[...] = jnp.zeros_like(y_ref)

    yp = y_ref[...]
    gain = pv_ref[0:1, :]
    normed = pv_ref[1:2, :] > 0.0
    roped = pv_ref[2:3, :] > 0.0
    ss = _dot((yp * yp).astype(BF16), bd_ref[...])
    rs = lax.rsqrt(ss * (1.0 / HEAD_DIM) + NORM_EPS)
    n = yp * (jnp.where(normed, rs, 1.0) * gain)
    cos = jnp.where(roped, cos_ref[...], 1.0)
    sin = jnp.where(roped, sin_ref[...], 0.0)
    w = n.shape[-1]
    lane = lax.broadcasted_iota(jnp.int32, n.shape, 1)
    partner = jnp.where((lane % 32) < 16, pltpu.roll(n, w - 16, 1), pltpu.roll(n, 16, 1))
    qkv_ref[...] = (n * cos + partner * sin).astype(BF16)

    y = _dot(h_ref[...], w_ref[...])
    y_ref[...] = y

    @pl.when(j == pl.num_programs(1) - 1)
    def _():
        du_ref[...] = y


def _project(x2, mod4, row_of_tile, g_norm, w_cat, pvecs, cos_t, sin_t, table_block, bd, tm):
    M, D = x2.shape
    nblk = w_cat.shape[1] // MIX_W
    nq = nblk - 1
    mod_spec = lambda k: pl.BlockSpec((None, None, 1, D), lambda i, j: (row_of_tile(i), k, 0, 0))
    return pl.pallas_call(
        _proj_kernel,
        grid=(M // tm, nblk),
        in_specs=[
            pl.BlockSpec((tm, D), lambda i, j: (i, 0)),
            pl.BlockSpec((1, D), lambda i, j: (0, 0)),
            mod_spec(0), mod_spec(1),
            pl.BlockSpec((D, MIX_W), lambda i, j: (0, j)),
            pl.BlockSpec((None, 8, MIX_W), lambda i, j: (jnp.maximum(j - 1, 0), 0, 0)),
            pl.BlockSpec((tm, MIX_W), lambda i, j: (table_block(i), 0)),
            pl.BlockSpec((tm, MIX_W), lambda i, j: (table_block(i), 0)),
            pl.BlockSpec((MIX_W, MIX_W), lambda i, j: (0, 0)),
        ],
        out_specs=[
            pl.BlockSpec((tm, MIX_W), lambda i, j: (i, jnp.maximum(j - 1, 0))),
            pl.BlockSpec((tm, MIX_W), lambda i, j: (i, 0)),
            pl.BlockSpec((tm, D), lambda i, j: (i, 0)),
        ],
        out_shape=[
            jax.ShapeDtypeStruct((M, nq * MIX_W), BF16),
            jax.ShapeDtypeStruct((M, MIX_W), F32),
            jax.ShapeDtypeStruct((M, D), BF16),
        ],
        scratch_shapes=[pltpu.VMEM((tm, MIX_W), F32)],
        compiler_params=_cparams(("parallel", "arbitrary")),
        name="in_proj",
    )(x2, g_norm, mod4, mod4, w_cat, pvecs, cos_t, sin_t, bd)


_QA, _KA, _VA = 0, 4, 8
_QB, _KB, _VB = 12, 16, 20
_QC = 24
_KC, _VC = 28, 30


def _softmax_pv(parts, extra_logit=None):
    s = jnp.concatenate([s for s, _ in parts], axis=1) if len(parts) > 1 else parts[0][0]
    m = s.max(axis=-1, keepdims=True)
    if extra_logit is not None:
        m = jnp.maximum(m, extra_logit)
    p = jnp.exp(s - m)
    l = p.sum(axis=-1, keepdims=True)
    if extra_logit is not None:
        l = l + jnp.exp(extra_logit - m)
    pb = p.astype(BF16)
    o = None
    start = 0
    for sp, v in parts:
        width = sp.shape[1]
        term = _dot(pb[:, start:start + width], v)
        o = term if o is None else o + term
        start += width
    return o / l


NA_GROUP = 4
NA_UNION = NA_GROUP + NA_ROWS


def _na_kernel(q_ref, k_ref, v_ref, kc_ref, vc_ref, bias_ref, o_ref, *, rows, units):
    g0 = pl.program_id(2) * units
    n_groups = rows // NA_GROUP
    tq = NA_GROUP * GRID_W
    kc = kc_ref[...]
    vc = vc_ref[...]
    for u in range(units):
        g = g0 + u
        start = jnp.clip(NA_GROUP * g - NA_ROWS // 2, 0, rows - NA_UNION)
        variant = jnp.where(g == 0, 0, jnp.where(g == n_groups - 1, 2, 1))
        qrows = slice(u * tq, (u + 1) * tq)
        krows = pl.ds(pl.multiple_of(start * GRID_W, GRID_W), NA_UNION * GRID_W)
        qs = _stack_halves(q_ref[qrows, :])
        s = _dot_nt(qs, k_ref[krows, :]) + bias_ref[variant]
        sc = _dot_nt(qs, kc)
        o = _softmax_pv([(s, v_ref[krows, :]), (sc, vc)])
        o_ref[qrows, :] = _merge_halves(o, tq).astype(o_ref.dtype)


def _na_attention(qkv, qkv_c, bias, B, S, Lc):
    rows = S // GRID_W
    assert rows % NA_GROUP == 0 and rows >= NA_UNION
    n_groups = rows // NA_GROUP
    units = 4 if n_groups % 4 == 0 else 1
    nstep = n_groups // units
    tq = units * NA_GROUP * GRID_W
    npair = MIX_W // LANES
    return pl.pallas_call(
        functools.partial(_na_kernel, rows=rows, units=units),
        grid=(B, npair, nstep),
        in_specs=[
            pl.BlockSpec((tq, LANES), lambda b, p, r: (b * nstep + r, _QA + p)),
            pl.BlockSpec((S, LANES), lambda b, p, r: (b, _KA + p)),
            pl.BlockSpec((S, LANES), lambda b, p, r: (b, _VA + p)),
            pl.BlockSpec((Lc, LANES), lambda b, p, r: (b, _KA + p)),
            pl.BlockSpec((Lc, LANES), lambda b, p, r: (b, _VA + p)),
            pl.BlockSpec((None,) + bias.shape[1:], lambda b, p, r: (p, 0, 0, 0)),
        ],
        out_specs=pl.BlockSpec((tq, LANES), lambda b, p, r: (b * nstep + r, p)),
        out_shape=jax.ShapeDtypeStruct((B * S, MIX_W), BF16),
        compiler_params=_cparams(("parallel", "parallel", "arbitrary")),
        name="na_attention",
    )(qkv, qkv, qkv, qkv_c, qkv_c, bias)


def _na_bias_kernel(rpb_ref, onehot_ref, valid_ref, o_ref):
    r = rpb_ref[...]
    hi = r.astype(BF16)
    r1 = r - hi.astype(F32)
    mid = r1.astype(BF16)
    lo = (r1 - mid.astype(F32)).astype(BF16)
    oh = onehot_ref[...]
    t = _dot(hi, oh) + _dot(mid, oh) + _dot(lo, oh)
    o_ref[...] = jnp.where(valid_ref[...] > 0.0, t, MASK_VALUE)


def _na_window_pattern(rows, g):
    start_u = np.clip(NA_GROUP * g - NA_ROWS // 2, 0, rows - NA_UNION)
    r = NA_GROUP * g + np.arange(NA_GROUP)[:, None]
    start_r = np.clip(r - NA_ROWS // 2, 0, rows - NA_ROWS)
    kr = start_u + np.arange(NA_UNION)[None, :]
    valid = (kr >= start_r) & (kr < start_r + NA_ROWS)
    return np.where(valid, kr - r + (NA_ROWS - 1), 0), valid


def _na_bias_table(rpb, rows):
    H, n_dr, n_dc = rpb.shape
    n_groups = rows // NA_GROUP
    patterns = [_na_window_pattern(rows, g) for g in (0, 1, n_groups - 1)]
    for g in range(1, n_groups - 1):
        dr_g, valid_g = _na_window_pattern(rows, g)
        assert (dr_g == patterns[1][0]).all() and (valid_g == patterns[1][1]).all()
    col = np.arange(GRID_W)
    col_start = np.clip(col - NA_COLS // 2, 0, GRID_W - NA_COLS)
    valid = (col[None, :] >= col_start[:, None]) & (col[None, :] < col_start[:, None] + NA_COLS)
    dc = np.clip(col[None, :] - col[:, None], -(NA_COLS - 1), NA_COLS - 1) + (NA_COLS - 1)
    onehot = (np.arange(LANES)[:, None] == dc.reshape(1, -1)).astype(np.float32)
    n_rows = -(-(H * n_dr) // LANES) * LANES
    rpb_rows = jnp.zeros((n_rows, LANES), F32).at[:H * n_dr, :n_dc].set(rpb.reshape(H * n_dr, n_dc).astype(F32))
    t = pl.pallas_call(
        _na_bias_kernel,
        out_shape=jax.ShapeDtypeStruct((n_rows, GRID_W * GRID_W), F32),
        name="na_bias",
    )(rpb_rows, jnp.asarray(onehot, BF16), jnp.asarray(valid.reshape(1, -1), F32))
    t = t[:H * n_dr].reshape(H, n_dr, GRID_W, GRID_W)
    masked = jnp.full((H, GRID_W, GRID_W), MASK_VALUE, F32)
    variants = []
    for dr, ok in patterns:
        slabs = [jnp.stack([t[:, int(dr[rr, j])] if ok[rr, j] else masked for j in range(NA_UNION)], axis=2)
                 for rr in range(NA_GROUP)]
        variants.append(jnp.stack(slabs, axis=1))
    win = jnp.stack(variants, axis=1)
    win = win.reshape(H // 2, 2, len(variants), NA_GROUP * GRID_W, NA_UNION * GRID_W)
    return win.transpose(0, 2, 1, 3, 4).reshape(H // 2, len(variants), 2 * NA_GROUP * GRID_W, NA_UNION * GRID_W)


def _lambda_full(lamp_ref, lam_init):
    lp = lamp_ref[...]
    a = jnp.sum(lp[0:1] * lp[1:2], axis=-1, keepdims=True)
    b = jnp.sum(lp[2:3] * lp[3:4], axis=-1, keepdims=True)
    return jnp.exp(a) - jnp.exp(b) + lam_init


def _subln(o, g, lam_init):
    ms = jnp.mean(o * o, axis=-1, keepdims=True)
    return o * lax.rsqrt(ms + NORM_EPS) * g * (1.0 - lam_init)


def _online_step(s, v, m, l, acc):
    m_new = jnp.maximum(m, s.max(axis=-1, keepdims=True))
    alpha = jnp.exp(m - m_new)
    p = jnp.exp(s - m_new)
    l = alpha * l + p.sum(axis=-1, keepdims=True)
    acc = alpha * acc + _dot(p.astype(BF16), v)
    return m_new, l, acc


def _diff_kernel(q_ref, k_ref, v_ref, kc_ref, vc_ref, lamp_ref, g_ref, o_ref, *, lam_init, tk):
    tq = q_ref.shape[0]
    S = k_ref.shape[0]
    nh = q_ref.shape[1] // LANES
    heads = [slice(h * LANES, (h + 1) * LANES) for h in range(nh)]
    qs = [_stack_halves(q_ref[:, hs]) for hs in heads]

    def body(kb, carry):
        rows = pl.ds(pl.multiple_of(kb * tk, tk), tk)
        return tuple(_online_step(_dot_nt(qs[h], k_ref[rows, hs]), v_ref[rows, hs], *carry[h])
                     for h, hs in enumerate(heads))

    init = (jnp.full((2 * tq, 1), MASK_VALUE, F32), jnp.zeros((2 * tq, 1), F32),
            jnp.zeros((2 * tq, LANES), F32))
    carry = lax.fori_loop(0, S // tk, body, (init,) * nh)
    lam = _lambda_full(lamp_ref, lam_init)
    for h, hs in enumerate(heads):
        m, l, acc = _online_step(_dot_nt(qs[h], kc_ref[:, hs]), vc_ref[:, hs], *carry[h])
        on = acc / l
        o = on[:tq] - lam * on[tq:]
        o_ref[:, hs] = _subln(o, g_ref[...], lam_init).astype(o_ref.dtype)


def _diff_attention(qkv, qkv_c, lamp, subln_g, lam_init, B, S, Lc):
    tq = 256
    nq = S // tq
    hw = 2 * LANES
    ngrp = MIX_W // hw
    qb, kb, vb = _QB * LANES // hw, _KB * LANES // hw, _VB * LANES // hw
    return pl.pallas_call(
        functools.partial(_diff_kernel, lam_init=lam_init, tk=512),
        grid=(B, ngrp, nq),
        in_specs=[
            pl.BlockSpec((tq, hw), lambda b, h, i: (b * nq + i, qb + h)),
            pl.BlockSpec((S, hw), lambda b, h, i: (b, kb + h)),
            pl.BlockSpec((S, hw), lambda b, h, i: (b, vb + h)),
            pl.BlockSpec((Lc, hw), lambda b, h, i: (b, kb + h)),
            pl.BlockSpec((Lc, hw), lambda b, h, i: (b, vb + h)),
            pl.BlockSpec((4, HEAD_DIM), lambda b, h, i: (0, 0)),
            pl.BlockSpec((1, LANES), lambda b, h, i: (0, 0)),
        ],
        out_specs=pl.BlockSpec((tq, hw), lambda b, h, i: (b * nq + i, h)),
        out_shape=jax.ShapeDtypeStruct((B * S, MIX_W), BF16),
        compiler_params=_cparams(("parallel", "parallel", "arbitrary")),
        name="diff_attention",
    )(qkv, qkv, qkv, qkv_c, qkv_c, lamp, subln_g)


def _stack_group(q4):
    return jnp.concatenate([_stack_halves(q4[:, :LANES]), _stack_halves(q4[:, LANES:])], axis=0)


def _merge_group(o, t):
    return jnp.concatenate([_merge_halves(o[:2 * t], t), _merge_halves(o[2 * t:], t)], axis=-1)


def _sink_column(sink_ref, c, t):
    group = sink_ref.shape[0] // SWA_KV_HEADS
    return jnp.concatenate([jnp.full((t, 1), sink_ref[c * group + g], F32) for g in range(group)], axis=0)


def _swa_kernel(sink_ref, q_ref, k_ref, v_ref, kc_ref, vc_ref, mask_ref, o_ref, *, qb):
    c = pl.program_id(1)
    n0 = pl.program_id(2) * qb
    W = SWA_WINDOW
    S = k_ref.shape[0]
    nb = S // W
    kc = kc_ref[...]
    vc = vc_ref[...]
    sink_col = _sink_column(sink_ref, c, W)
    group = sink_ref.shape[0] // SWA_KV_HEADS
    for t in range(qb):
        n = n0 + t
        ks = pl.multiple_of(jnp.clip((n - 1) * W, 0, S - 3 * W), W)
        krows = pl.ds(ks, 3 * W)
        variant = jnp.where(n == 0, 0, jnp.where(n == nb - 1, 2, 1))
        mask = mask_ref[variant]
        qs = _stack_group(q_ref[t * W:(t + 1) * W, :])
        s = _dot_nt(qs, k_ref[krows, :]) + jnp.concatenate([mask] * group, axis=0)
        sc = _dot_nt(qs, kc)
        o = _softmax_pv([(s, v_ref[krows, :]), (sc, vc)], extra_logit=sink_col)
        o_ref[t * W:(t + 1) * W, :] = _merge_group(o, W).astype(o_ref.dtype)


def _swa_mask_table():
    W = SWA_WINDOW
    i = np.arange(W)[:, None]
    j = np.arange(3 * W)[None, :]
    tabs = [np.where(np.abs(j - shift - i) <= W, 0.0, MASK_VALUE) for shift in (0, W, 2 * W)]
    return jnp.asarray(np.stack(tabs), F32)


def _swa_attention(qkv, qkv_c, sink, mask, B, S, Lc):
    W = SWA_WINDOW
    nb = S // W
    qb = 8 if nb % 8 == 0 else 1
    nqb = nb // qb
    return pl.pallas_call(
        functools.partial(_swa_kernel, qb=qb),
        grid=(B, SWA_KV_HEADS, nqb),
        in_specs=[
            pl.BlockSpec(memory_space=pltpu.SMEM),
            pl.BlockSpec((qb * W, 2 * LANES), lambda b, c, n: (b * nqb + n, _QC // 2 + c)),
            pl.BlockSpec((S, LANES), lambda b, c, n: (b, _KC + c)),
            pl.BlockSpec((S, LANES), lambda b, c, n: (b, _VC + c)),
            pl.BlockSpec((Lc, LANES), lambda b, c, n: (b, _KC + c)),
            pl.BlockSpec((Lc, LANES), lambda b, c, n: (b, _VC + c)),
            pl.BlockSpec(mask.shape, lambda b, c, n: (0, 0, 0)),
        ],
        out_specs=pl.BlockSpec((qb * W, 2 * LANES), lambda b, c, n: (b * nqb + n, c)),
        out_shape=jax.ShapeDtypeStruct((B * S, MIX_W), BF16),
        compiler_params=_cparams(("parallel", "parallel", "arbitrary")),
        name="swa_attention",
    )(sink, qkv, qkv, qkv, qkv_c, qkv_c, mask)


def _ctx_attn_kernel(sink_ref, qkv_ref, lamp_ref, g_ref, oa_ref, ob_ref, os_ref, *, lam_init):
    Lc = qkv_ref.shape[0]

    def cols(blk, width=LANES):
        return qkv_ref[:, blk * LANES: blk * LANES + width]

    for p in range(MIX_W // LANES):
        qs = _stack_halves(cols(_QA + p))
        o = _softmax_pv([(_dot_nt(qs, cols(_KA + p)), cols(_VA + p))])
        oa_ref[:, p * LANES:(p + 1) * LANES] = _merge_halves(o, Lc).astype(oa_ref.dtype)

    lam = _lambda_full(lamp_ref, lam_init)
    for h in range(MIX_W // LANES):
        qs = _stack_halves(cols(_QB + h))
        on = _softmax_pv([(_dot_nt(qs, cols(_KB + h)), cols(_VB + h))])
        o = on[:Lc] - lam * on[Lc:]
        ob_ref[:, h * LANES:(h + 1) * LANES] = _subln(o, g_ref[...], lam_init).astype(ob_ref.dtype)

    for c in range(SWA_KV_HEADS):
        qs = _stack_group(cols(_QC + 2 * c, 2 * LANES))
        o = _softmax_pv([(_dot_nt(qs, cols(_KC + c)), cols(_VC + c))],
                        extra_logit=_sink_column(sink_ref, c, Lc))
        os_ref[:, c * 2 * LANES:(c + 1) * 2 * LANES] = _merge_group(o, Lc).astype(os_ref.dtype)


def _ctx_attention(qkv_c, sink, lamp, subln_g, lam_init, B, Lc):
    NQ = qkv_c.shape[1]
    out = jax.ShapeDtypeStruct((B * Lc, MIX_W), BF16)
    ospec = pl.BlockSpec((Lc, MIX_W), lambda b: (b, 0))
    return pl.pallas_call(
        functools.partial(_ctx_attn_kernel, lam_init=lam_init),
        grid=(B,),
        in_specs=[
            pl.BlockSpec(memory_space=pltpu.SMEM),
            pl.BlockSpec((Lc, NQ), lambda b: (b, 0)),
            pl.BlockSpec((4, HEAD_DIM), lambda b: (0, 0)),
            pl.BlockSpec((1, LANES), lambda b: (0, 0)),
        ],
        out_specs=[ospec, ospec, ospec],
        out_shape=[out, out, out],
        compiler_params=_cparams(("parallel",)),
        name="ctx_attention",
    )(sink, qkv_c, lamp, subln_g)


def _shift_down(x, k, t_idx):
    return jnp.where(t_idx >= k, pltpu.roll(x, k, 0), 0.0)


def _shift_up(x, k, t_idx):
    T = x.shape[0]
    return jnp.where(t_idx < T - k, pltpu.roll(x, T - k, 0), 0.0)


def _pool_kernel(u_ref, w_ref, scale_ref, o_ref):
    T = u_ref.shape[0]
    pg = u_ref.shape[1] // len(POOL_WINDOWS)
    t_idx = lax.broadcasted_iota(jnp.int32, (T, pg), 0)
    for g, w in enumerate(POOL_WINDOWS):
        u = u_ref[:, g * pg:(g + 1) * pg]
        back = u
        fwd = u
        span = 1
        while span < w // 2:
            back = back + _shift_down(back, span, t_idx)
            fwd = fwd + _shift_up(fwd, span, t_idx)
            span *= 2
        win = _shift_down(back, 1, t_idx) + fwd
        lo = jnp.clip(t_idx - w // 2, 0, T)
        hi = jnp.clip(t_idx - w // 2 + w, 0, T)
        d = (win / (hi - lo).astype(F32) - u).astype(BF16)
        mixed = _dot(d, w_ref[g])
        o_ref[:, g * pg:(g + 1) * pg] = (mixed * scale_ref[:, g * pg:(g + 1) * pg]).astype(o_ref.dtype)


def _pool_mixer(du, w_grp, scale, nseq, T):
    return pl.pallas_call(
        _pool_kernel,
        grid=(nseq,),
        in_specs=[
            pl.BlockSpec((T, MIX_W), lambda b: (b, 0)),
            pl.BlockSpec(w_grp.shape, lambda b: (0, 0, 0)),
            pl.BlockSpec((1, MIX_W), lambda b: (0, 0)),
        ],
        out_specs=pl.BlockSpec((T, MIX_W), lambda b: (b, 0)),
        out_shape=jax.ShapeDtypeStruct((nseq * T, MIX_W), BF16),
        compiler_params=_cparams(("parallel",)),
        name="pool_mixer",
    )(du, w_grp, scale)


def _merge_kernel(h_ref, oa_ref, ob_ref, os_ref, od_ref, wg_ref, wb_ref, acc_ref, f32_ref):
    tn = acc_ref.shape[1]
    h = h_ref[...]
    for n, o_ref in enumerate((oa_ref, ob_ref, os_ref, od_ref)):
        gate = jax.nn.sigmoid(_dot(h, wg_ref[:, n * tn:(n + 1) * tn]))
        term = gate * _dot(o_ref[...], wb_ref[n])
        if n == 0:
            f32_ref[...] = term
        else:
            f32_ref[...] += term
    acc_ref[...] = f32_ref[...].astype(acc_ref.dtype)


def _merge(h, outs, w_gate_r, w_br, tm, tn):
    M, D = h.shape
    ospec = pl.BlockSpec((tm, MIX_W), lambda i, j: (i, 0))
    return pl.pallas_call(
        _merge_kernel,
        grid=(M // tm, D // tn),
        in_specs=[
            pl.BlockSpec((tm, D), lambda i, j: (i, 0)),
            ospec, ospec, ospec, ospec,
            pl.BlockSpec((D, N_BRANCH * tn), lambda i, j: (0, j)),
            pl.BlockSpec((N_BRANCH, MIX_W, tn), lambda i, j: (0, 0, j)),
        ],
        out_specs=pl.BlockSpec((tm, tn), lambda i, j: (i, j)),
        out_shape=jax.ShapeDtypeStruct((M, D), BF16),
        scratch_shapes=[pltpu.VMEM((tm, tn), F32)],
        compiler_params=_cparams(("parallel", "arbitrary")),
        name="gated_merge",
    )(h, *outs, w_gate_r, w_br)


def _outproj_kernel(a_ref, w_ref, x_ref, g_ref, o_ref, *, tc):
    a = a_ref[...]
    for c in range(o_ref.shape[1] // tc):
        cs = slice(c * tc, (c + 1) * tc)
        o_ref[:, cs] = x_ref[:, cs] + g_ref[:, cs] * _dot(a, w_ref[:, cs])


def _out_project(acc, w_o, x2, mod4, row_of_tile, tm):
    M, D = x2.shape
    return pl.pallas_call(
        functools.partial(_outproj_kernel, tc=512),
        grid=(M // tm,),
        in_specs=[
            pl.BlockSpec((tm, D), lambda i: (i, 0)),
            pl.BlockSpec((D, D), lambda i: (0, 0), pipeline_mode=pl.Buffered(1)),
            pl.BlockSpec((tm, D), lambda i: (i, 0)),
            pl.BlockSpec((None, None, 1, D), lambda i: (row_of_tile(i), 2, 0, 0)),
        ],
        out_specs=pl.BlockSpec((tm, D), lambda i: (i, 0)),
        out_shape=jax.ShapeDtypeStruct((M, D), F32),
        compiler_params=_cparams(("parallel",)),
        name="out_proj",
    )(acc, w_o, x2, mod4)


def _ffn_kernel(x_ref, g_ref, sh_ref, sc_ref, gate_ref, wg_ref, wu_ref, wd_ref, o_ref, h_ref, *, tc):
    f = pl.program_id(1)
    D = o_ref.shape[1]

    @pl.when(f == 0)
    def _():
        _modnorm_into(h_ref, x_ref, g_ref, sh_ref, sc_ref)
        o_ref[...] = jnp.zeros_like(o_ref)

    h = h_ref[...]
    a = _dot(h, wg_ref[...])
    u = _dot(h, wu_ref[...])
    act = (a * jax.nn.sigmoid(a) * u).astype(BF16)
    for c in range(D // tc):
        cs = slice(c * tc, (c + 1) * tc)
        o_ref[:, cs] += _dot(act, wd_ref[:, cs])

    @pl.when(f == pl.num_programs(1) - 1)
    def _():
        o_ref[...] = x_ref[...] + gate_ref[...] * o_ref[...]


def _ffn(x2, mod4, row_of_tile, g_norm, wg, wu, wd, tm, tf):
    M, D = x2.shape
    F = wg.shape[1]
    mod_spec = lambda k: pl.BlockSpec((None, None, 1, D), lambda i, f: (row_of_tile(i), k, 0, 0))
    return pl.pallas_call(
        functools.partial(_ffn_kernel, tc=512),
        grid=(M // tm, F // tf),
        in_specs=[
            pl.BlockSpec((tm, D), lambda i, f: (i, 0)),
            pl.BlockSpec((1, D), lambda i, f: (0, 0)),
            mod_spec(3), mod_spec(4), mod_spec(5),
            pl.BlockSpec((D, tf), lambda i, f: (0, f)),
            pl.BlockSpec((D, tf), lambda i, f: (0, f)),
            pl.BlockSpec((tf, D), lambda i, f: (f, 0)),
        ],
        out_specs=pl.BlockSpec((tm, D), lambda i, f: (i, 0)),
        out_shape=jax.ShapeDtypeStruct((M, D), F32),
        scratch_shapes=[pltpu.VMEM((tm, D), BF16)],
        compiler_params=_cparams(("parallel", "arbitrary")),
        name="swiglu_ffn",
    )(x2, g_norm, mod4, mod4, mod4, wg, wu, wd)


def _rope_tables(S):
    t = jnp.arange(S)
    pos = jnp.stack([t // GRID_W, t % GRID_W], axis=-1).astype(F32)
    n_freq = HEAD_DIM // 4
    inv = ROPE_BASE ** (-jnp.arange(n_freq, dtype=F32) / n_freq)
    ang = pos[:, :, None] * inv
    cos, sin = jnp.cos(ang), jnp.sin(ang)
    cos_h = jnp.concatenate([cos, cos], axis=-1).reshape(S, HEAD_DIM)
    sin_h = jnp.concatenate([-sin, sin], axis=-1).reshape(S, HEAD_DIM)
    reps = MIX_W // HEAD_DIM
    return jnp.tile(cos_h, (1, reps)), jnp.tile(sin_h, (1, reps))


def _split_w_in(w_in_l):
    sizes = (MIX_W,) * 7 + (SWA_KV_HEADS * HEAD_DIM,) * 2 + (MIX_W,)
    offs = [0]
    for s in sizes:
        offs.append(offs[-1] + s)
    parts = [w_in_l[:, offs[i]:offs[i + 1]] for i in range(len(sizes))]
    return parts, w_in_l[:, offs[-1]:]


def _dup_heads(w):
    D = w.shape[0]
    return jnp.repeat(w.reshape(D, SWA_KV_HEADS, 1, HEAD_DIM), 2, axis=2).reshape(D, 2 * SWA_KV_HEADS * HEAD_DIM)


def _tile_heads(g, scale=1.0):
    return jnp.tile(g.astype(F32) * scale, MIX_W // HEAD_DIM)


def _proj_column_vectors(gains):
    rope_blocks = (3, 4, 6, 7)
    half = MIX_W // 2
    rows = []
    for j, g in enumerate(gains):
        gain = jnp.ones((MIX_W,), F32) if g is None else g
        normed = jnp.zeros((MIX_W,), F32) if g is None else jnp.ones((MIX_W,), F32)
        roped = jnp.ones((MIX_W,), F32) if j in rope_blocks else jnp.zeros((MIX_W,), F32)
        if j == 7:
            keep = (jnp.arange(MIX_W) < half).astype(F32)
            gain = jnp.where(keep > 0, gain, 1.0)
            normed, roped = normed * keep, roped * keep
        rows.append(jnp.stack([gain, normed, roped] + [jnp.zeros((MIX_W,), F32)] * 5))
    return jnp.stack(rows)


def _pick_tile(M, unit, target):
    t = min(target, M)
    while M % t or (t > unit and t % unit) or (t < unit and unit % t):
        t //= 2
    return t


def kernel(x, c, ctx, c_ctx, w_ada, b_ada, norm_mix, norm_ffn, w_in, a_q_norm, a_k_norm, a_rpb, b_q_norm, b_k_norm, b_lam_q1, b_lam_k1, b_lam_q2, b_lam_k2, b_subln, c_q_norm, c_k_norm, c_sink, d_w, d_scale, w_branch, w_out, w_ffn_gate, w_ffn_up, w_ffn_down):
    B, S, D = x.shape
    Lc = ctx.shape[1]
    L = w_ada.shape[0]
    assert D == N_BRANCH * MIX_W and S % GRID_W == 0 and S // GRID_W >= NA_ROWS and S >= 3 * SWA_WINDOW

    tm = _pick_tile(S, S, 512)
    tm_proj = _pick_tile(S, S, 1024)
    tm_merge = _pick_tile(B * S, 8, 1024)
    tmc = _pick_tile(B * Lc, 8, 512)
    lat_row = lambda i: i // (S // tm)
    lat_row_proj = lambda i: i // (S // tm_proj)
    ctx_row = lambda i: B
    n_mod_rows = -(-(B + 1) // 8) * 8

    cc = jnp.zeros((n_mod_rows, D), F32).at[:B].set(c).at[B].set(c_ctx)
    mod = _modulation(cc, w_ada, b_ada).reshape(L, n_mod_rows, 6, 1, D)

    cos_t, sin_t = _rope_tables(S)
    cos_id = jnp.ones((tmc, MIX_W), F32)
    sin_id = jnp.zeros((tmc, MIX_W), F32)
    lane = jnp.arange(MIX_W)
    bd = (lane[:, None] // HEAD_DIM == lane[None, :] // HEAD_DIM).astype(BF16)
    swa_mask = _swa_mask_table()

    x2 = x.reshape(B * S, D)
    c2 = ctx.reshape(B * Lc, D)
    tn_merge = 256

    for l in range(L):
        last = l == L - 1
        lam_init = 0.8 - 0.6 * math.exp(-0.3 * l)
        mod_l = mod[l]

        (aq, ak, av, bq, bk, bv, sq, sk, sv, du), w_gl = _split_w_in(w_in[l])
        w_cat = jnp.concatenate([aq, ak, av, bq, bk, bv, sq, _dup_heads(sk), _dup_heads(sv), du], axis=1).astype(BF16)
        pvecs = _proj_column_vectors([
            _tile_heads(a_q_norm[l], ATTN_SCALE), _tile_heads(a_k_norm[l]), None,
            _tile_heads(b_q_norm[l], ATTN_SCALE), _tile_heads(b_k_norm[l]), None,
            _tile_heads(c_q_norm[l], ATTN_SCALE),
            _tile_heads(c_k_norm[l]),
        ])
        w_gate_r = (w_gl.reshape(D, N_BRANCH, D // tn_merge, tn_merge).transpose(0, 2, 1, 3)
                    .reshape(D, N_BRANCH * D).astype(BF16))
        w_br = w_branch[l].astype(BF16)
        w_o = w_out[l].astype(BF16)
        wg, wu, wd = w_ffn_gate[l].astype(BF16), w_ffn_up[l].astype(BF16), w_ffn_down[l].astype(BF16)
        g_mix_norm = norm_mix[l].reshape(1, D)
        g_ffn_norm = norm_ffn[l].reshape(1, D)
        lamp = jnp.stack([b_lam_q1[l], b_lam_k1[l], b_lam_q2[l], b_lam_k2[l]]).astype(F32)
        subln_g = b_subln[l].reshape(1, LANES).astype(F32)
        sink = c_sink[l].astype(F32)
        w_pool = d_w[l].astype(BF16)
        pool_scale = d_scale[l].reshape(1, MIX_W).astype(F32)

        qkv, du_l, h_l = _project(x2, mod_l, lat_row_proj, g_mix_norm, w_cat, pvecs, cos_t, sin_t,
                                  lambda i: i % (S // tm_proj), bd, tm_proj)
        qkv_c, du_c, h_c = _project(c2, mod_l, ctx_row, g_mix_norm, w_cat, pvecs, cos_id, sin_id,
                                    lambda i: 0, bd, tmc)

        o_a = _na_attention(qkv, qkv_c, _na_bias_table(a_rpb[l], S // GRID_W), B, S, Lc)
        o_b = _diff_attention(qkv, qkv_c, lamp, subln_g, lam_init, B, S, Lc)
        o_s = _swa_attention(qkv, qkv_c, sink, swa_mask, B, S, Lc)
        o_d = _pool_mixer(du_l, w_pool, pool_scale, B, S)
        acc = _merge(h_l, (o_a, o_b, o_s, o_d), w_gate_r, w_br, tm_merge, tn_merge)
        x2 = _out_project(acc, w_o, x2, mod_l, lat_row, tm)

        if not last:
            oc_a, oc_b, oc_s = _ctx_attention(qkv_c, sink, lamp, subln_g, lam_init, B, Lc)
            oc_d = _pool_mixer(du_c, w_pool, pool_scale, B, Lc)
            acc_c = _merge(h_c, (oc_a, oc_b, oc_s, oc_d), w_gate_r, w_br, tmc, tn_merge)
            c2 = _out_project(acc_c, w_o, c2, mod_l, ctx_row, tmc)
            c2 = _ffn(c2, mod_l, ctx_row, g_ffn_norm, wg, wu, wd, tmc, 512)

        x2 = _ffn(x2, mod_l, lat_row, g_ffn_norm, wg, wu, wd, tm, 512)

    return x2.reshape(B, S, D)
```

```python
import functools
import math

import numpy as np
import jax
import jax.numpy as jnp
from jax import lax
from jax.experimental import pallas as pl
from jax.experimental.pallas import tpu as pltpu

F32 = jnp.float32
BF16 = jnp.bfloat16

HEAD_DIM = 64
LANES = 128
GRID_W = 64
N_BRANCH = 4
NA_ROWS = 8
NA_COLS = 16
SWA_WINDOW = 128
SWA_KV_HEADS = 2
POOL_WINDOWS = (2, 4, 8, 16)
ROPE_BASE = 10000.0
NORM_EPS = 1e-6
MASK_VALUE = -1e30
ATTN_SCALE = HEAD_DIM ** -0.5
MIX_W = 512
VMEM_LIMIT = 56 * 1024 * 1024


def _cparams(sem):
    return pltpu.CompilerParams(dimension_semantics=sem, vmem_limit_bytes=VMEM_LIMIT)


def _dot(a, b):
    return jnp.dot(a, b, preferred_element_type=F32)


def _dot_nt(a, b):
    return lax.dot_general(a, b, (((1,), (1,)), ((), ())), preferred_element_type=F32)


def _lane_lo(shape):
    return lax.broadcasted_iota(jnp.int32, shape, len(shape) - 1) % LANES < HEAD_DIM


def _stack_halves(q2):
    lo = _lane_lo(q2.shape)
    zero = jnp.zeros_like(q2)
    return jnp.concatenate([jnp.where(lo, q2, zero), jnp.where(lo, zero, q2)], axis=0)


def _merge_halves(o, t):
    lo = _lane_lo((t, LANES))
    return jnp.where(lo, o[:t], o[t:])


def _mod_kernel(c_ref, w_ref, b_ref, o_ref):
    c = c_ref[...]
    act = (c * jax.nn.sigmoid(c)).astype(BF16)
    o_ref[...] = _dot(act, w_ref[...].astype(BF16)) + b_ref[...]


def _modulation(cc, w_ada, b_ada):
    L, D, N = w_ada.shape
    R = cc.shape[0]
    tn = 1024
    return pl.pallas_call(
        _mod_kernel,
        grid=(L, N // tn),
        in_specs=[
            pl.BlockSpec((R, D), lambda l, j: (0, 0)),
            pl.BlockSpec((None, D, tn), lambda l, j: (l, 0, j)),
            pl.BlockSpec((None, 1, tn), lambda l, j: (l, 0, j)),
        ],
        out_specs=pl.BlockSpec((None, R, tn), lambda l, j: (l, 0, j)),
        out_shape=jax.ShapeDtypeStruct((L, R, N), F32),
        compiler_params=_cparams(("parallel", "parallel")),
        name="adaln_mod",
    )(cc, w_ada, b_ada.reshape(L, 1, N))


def _modnorm_into(dst_ref, x_ref, g_ref, sh_ref, sc_ref, chunk=128):
    tm = x_ref.shape[0]
    sh = sh_ref[...]
    gain = g_ref[...] * (1.0 + sc_ref[...])

    def body(r, carry):
        rows = pl.ds(pl.multiple_of(r * chunk, chunk), chunk)
        xf = x_ref[rows, :]
        ms = jnp.mean(xf * xf, axis=-1, keepdims=True)
        dst_ref[rows, :] = (xf * lax.rsqrt(ms + NORM_EPS) * gain + sh).astype(dst_ref.dtype)
        return carry

    lax.fori_loop(0, tm // chunk, body, 0)


def _proj_kernel(x_ref, g_ref, sh_ref, sc_ref, w_ref, pv_ref, cos_ref, sin_ref, bd_ref,
                 qkv_ref, du_ref, h_ref, y_ref):
    j = pl.program_id(1)

    @pl.when(j == 0)
    def _():
        _modnorm_into(h_ref, x_ref, g_ref, sh_ref, sc_ref)
        y_ref[...] = jnp.zeros_like(y_ref)

    yp = y_ref[...]
    gain = pv_ref[0:1, :]
    normed = pv_ref[1:2, :] > 0.0
    roped = pv_ref[2:3, :] > 0.0
    y2 = (yp * yp).astype(BF16)
    hw = bd_ref.shape[0]
    ss = jnp.concatenate([_dot(y2[:, c * hw:(c + 1) * hw], bd_ref[...]) for c in range(y2.shape[1] // hw)], axis=1)
    rs = lax.rsqrt(ss * (1.0 / HEAD_DIM) + NORM_EPS)
    n = yp * (jnp.where(normed, rs, 1.0) * gain)
    cos = jnp.where(roped, cos_ref[...], 1.0)
    sin = jnp.where(roped, sin_ref[...], 0.0)
    w = n.shape[-1]
    lane = lax.broadcasted_iota(jnp.int32, n.shape, 1)
    partner = jnp.where((lane % 32) < 16, pltpu.roll(n, w - 16, 1), pltpu.roll(n, 16, 1))
    qkv_ref[...] = (n * cos + partner * sin).astype(BF16)

    y = _dot(h_ref[...], w_ref[...])
    y_ref[...] = y

    @pl.when(j == pl.num_programs(1) - 1)
    def _():
        du_ref[...] = y


def _project(x2, mod4, row_of_tile, g_norm, w_cat, pvecs, cos_t, sin_t, table_block, bd, tm):
    M, D = x2.shape
    nblk = w_cat.shape[1] // MIX_W
    nq = nblk - 1
    mod_spec = lambda k: pl.BlockSpec((None, None, 1, D), lambda i, j: (row_of_tile(i), k, 0, 0))
    return pl.pallas_call(
        _proj_kernel,
        grid=(M // tm, nblk),
        in_specs=[
            pl.BlockSpec((tm, D), lambda i, j: (i, 0)),
            pl.BlockSpec((1, D), lambda i, j: (0, 0)),
            mod_spec(0), mod_spec(1),
            pl.BlockSpec((D, MIX_W), lambda i, j: (0, j)),
            pl.BlockSpec((None, 8, MIX_W), lambda i, j: (jnp.maximum(j - 1, 0), 0, 0)),
            pl.BlockSpec((tm, MIX_W), lambda i, j: (table_block(i), 0)),
            pl.BlockSpec((tm, MIX_W), lambda i, j: (table_block(i), 0)),
            pl.BlockSpec(bd.shape, lambda i, j: (0, 0)),
        ],
        out_specs=[
            pl.BlockSpec((tm, MIX_W), lambda i, j: (i, jnp.maximum(j - 1, 0))),
            pl.BlockSpec((tm, MIX_W), lambda i, j: (i, 0)),
            pl.BlockSpec((tm, D), lambda i, j: (i, 0)),
        ],
        out_shape=[
            jax.ShapeDtypeStruct((M, nq * MIX_W), BF16),
            jax.ShapeDtypeStruct((M, MIX_W), F32),
            jax.ShapeDtypeStruct((M, D), BF16),
        ],
        scratch_shapes=[pltpu.VMEM((tm, MIX_W), F32)],
        compiler_params=_cparams(("parallel", "arbitrary")),
        name="in_proj",
    )(x2, g_norm, mod4, mod4, w_cat, pvecs, cos_t, sin_t, bd)


_QA, _KA, _VA = 0, 4, 8
_QB, _KB, _VB = 12, 16, 20
_QC = 24
_KC, _VC = 28, 30


def _softmax_pv(parts, extra_logit=None):
    s = jnp.concatenate([s for s, _ in parts], axis=1) if len(parts) > 1 else parts[0][0]
    m = s.max(axis=-1, keepdims=True)
    if extra_logit is not None:
        m = jnp.maximum(m, extra_logit)
    p = jnp.exp(s - m)
    l = p.sum(axis=-1, keepdims=True)
    if extra_logit is not None:
        l = l + jnp.exp(extra_logit - m)
    pb = p.astype(BF16)
    o = None
    start = 0
    for sp, v in parts:
        width = sp.shape[1]
        term = _dot(pb[:, start:start + width], v)
        o = term if o is None else o + term
        start += width
    return o / l


NA_GROUP = 4
NA_UNION = NA_GROUP + NA_ROWS


def _na_kernel(q_ref, k_ref, v_ref, kc_ref, vc_ref, bias_ref, o_ref, *, rows, units):
    g0 = pl.program_id(2) * units
    n_groups = rows // NA_GROUP
    tq = NA_GROUP * GRID_W
    kc = kc_ref[...]
    vc = vc_ref[...]
    scores = []
    for u in range(units):
        g = g0 + u
        start = jnp.clip(NA_GROUP * g - NA_ROWS // 2, 0, rows - NA_UNION)
        variant = jnp.where(g == 0, 0, jnp.where(g == n_groups - 1, 2, 1))
        krows = pl.ds(pl.multiple_of(start * GRID_W, GRID_W), NA_UNION * GRID_W)
        qs = _stack_halves(q_ref[u * tq:(u + 1) * tq, :])
        s = _dot_nt(qs, k_ref[krows, :]) + bias_ref[variant]
        sc = _dot_nt(qs, kc)
        scores.append((s, sc, krows))
    for u, (s, sc, krows) in enumerate(scores):
        o = _softmax_pv([(s, v_ref[krows, :]), (sc, vc)])
        o_ref[u * tq:(u + 1) * tq, :] = _merge_halves(o, tq).astype(o_ref.dtype)


def _na_attention(qkv, qkv_c, bias, B, S, Lc):
    rows = S // GRID_W
    assert rows % NA_GROUP == 0 and rows >= NA_UNION
    n_groups = rows // NA_GROUP
    units = 4 if n_groups % 4 == 0 else 1
    nstep = n_groups // units
    tq = units * NA_GROUP * GRID_W
    npair = MIX_W // LANES
    return pl.pallas_call(
        functools.partial(_na_kernel, rows=rows, units=units),
        grid=(B, npair, nstep),
        in_specs=[
            pl.BlockSpec((tq, LANES), lambda b, p, r: (b * nstep + r, _QA + p)),
            pl.BlockSpec((S, LANES), lambda b, p, r: (b, _KA + p)),
            pl.BlockSpec((S, LANES), lambda b, p, r: (b, _VA + p)),
            pl.BlockSpec((Lc, LANES), lambda b, p, r: (b, _KA + p)),
            pl.BlockSpec((Lc, LANES), lambda b, p, r: (b, _VA + p)),
            pl.BlockSpec((None,) + bias.shape[1:], lambda b, p, r: (p, 0, 0, 0)),
        ],
        out_specs=pl.BlockSpec((tq, LANES), lambda b, p, r: (b * nstep + r, p)),
        out_shape=jax.ShapeDtypeStruct((B * S, MIX_W), BF16),
        compiler_params=_cparams(("parallel", "parallel", "arbitrary")),
        name="na_attention",
    )(qkv, qkv, qkv, qkv_c, qkv_c, bias)


def _na_bias_kernel(rpb_ref, onehot_ref, valid_ref, o_ref):
    r = rpb_ref[...]
    hi = r.astype(BF16)
    r1 = r - hi.astype(F32)
    mid = r1.astype(BF16)
    lo = (r1 - mid.astype(F32)).astype(BF16)
    oh = onehot_ref[...]
    t = _dot(hi, oh) + _dot(mid, oh) + _dot(lo, oh)
    o_ref[...] = jnp.where(valid_ref[...] > 0.0, t, MASK_VALUE)


def _na_window_pattern(rows, g):
    start_u = np.clip(NA_GROUP * g - NA_ROWS // 2, 0, rows - NA_UNION)
    r = NA_GROUP * g + np.arange(NA_GROUP)[:, None]
    start_r = np.clip(r - NA_ROWS // 2, 0, rows - NA_ROWS)
    kr = start_u + np.arange(NA_UNION)[None, :]
    valid = (kr >= start_r) & (kr < start_r + NA_ROWS)
    return np.where(valid, kr - r + (NA_ROWS - 1), 0), valid


def _na_bias_table(rpb, rows):
    H, n_dr, n_dc = rpb.shape
    n_groups = rows // NA_GROUP
    patterns = [_na_window_pattern(rows, g) for g in (0, 1, n_groups - 1)]
    for g in range(1, n_groups - 1):
        dr_g, valid_g = _na_window_pattern(rows, g)
        assert (dr_g == patterns[1][0]).all() and (valid_g == patterns[1][1]).all()
    col = np.arange(GRID_W)
    col_start = np.clip(col - NA_COLS // 2, 0, GRID_W - NA_COLS)
    valid = (col[None, :] >= col_start[:, None]) & (col[None, :] < col_start[:, None] + NA_COLS)
    dc = np.clip(col[None, :] - col[:, None], -(NA_COLS - 1), NA_COLS - 1) + (NA_COLS - 1)
    onehot = (np.arange(LANES)[:, None] == dc.reshape(1, -1)).astype(np.float32)
    n_rows = -(-(H * n_dr) // LANES) * LANES
    rpb_rows = jnp.zeros((n_rows, LANES), F32).at[:H * n_dr, :n_dc].set(rpb.reshape(H * n_dr, n_dc).astype(F32))
    t = pl.pallas_call(
        _na_bias_kernel,
        out_shape=jax.ShapeDtypeStruct((n_rows, GRID_W * GRID_W), F32),
        name="na_bias",
    )(rpb_rows, jnp.asarray(onehot, BF16), jnp.asarray(valid.reshape(1, -1), F32))
    t = t[:H * n_dr].reshape(H, n_dr, GRID_W, GRID_W)
    masked = jnp.full((H, GRID_W, GRID_W), MASK_VALUE, F32)
    variants = []
    for dr, ok in patterns:
        slabs = [jnp.stack([t[:, int(dr[rr, j])] if ok[rr, j] else masked for j in range(NA_UNION)], axis=2)
                 for rr in range(NA_GROUP)]
        variants.append(jnp.stack(slabs, axis=1))
    win = jnp.stack(variants, axis=1)
    win = win.reshape(H // 2, 2, len(variants), NA_GROUP * GRID_W, NA_UNION * GRID_W)
    return win.transpose(0, 2, 1, 3, 4).reshape(H // 2, len(variants), 2 * NA_GROUP * GRID_W, NA_UNION * GRID_W)


def _lambda_full(lamp_ref, lam_init):
    lp = lamp_ref[...]
    a = jnp.sum(lp[0:1] * lp[1:2], axis=-1, keepdims=True)
    b = jnp.sum(lp[2:3] * lp[3:4], axis=-1, keepdims=True)
    return jnp.exp(a) - jnp.exp(b) + lam_init


def _subln(o, g, lam_init):
    ms = jnp.mean(o * o, axis=-1, keepdims=True)
    return o * lax.rsqrt(ms + NORM_EPS) * g * (1.0 - lam_init)


def _fold_lanes(x, op):
    out = x[:, :LANES]
    for b in range(1, x.shape[1] // LANES):
        out = op(out, x[:, b * LANES:(b + 1) * LANES])
    return out


def _diff_kernel(q_ref, k_ref, v_ref, kc_ref, vc_ref, lamp_ref, g_ref, o_ref, s_ref, *, lam_init, tk):
    tq = q_ref.shape[0]
    S = k_ref.shape[0]
    nkb = S // tk
    lam = _lambda_full(lamp_ref, lam_init)
    nh = q_ref.shape[1] // LANES
    heads = [slice(h * LANES, (h + 1) * LANES) for h in range(nh)]
    qs = [_stack_halves(q_ref[:, hs]) for hs in heads]
    lane_max, m, lane_sum, acc = [None] * nh, [None] * nh, [None] * nh, [None] * nh

    def pass1_block(h, kb):
        s = _dot_nt(qs[h], k_ref[kb * tk:(kb + 1) * tk, heads[h]])
        s_ref[h, kb] = s
        blk = _fold_lanes(s, jnp.maximum)
        lane_max[h] = blk if lane_max[h] is None else jnp.maximum(lane_max[h], blk)

    def pass1_finish(h):
        sc = _dot_nt(qs[h], kc_ref[:, heads[h]])
        m[h] = jnp.maximum(lane_max[h], _fold_lanes(sc, jnp.maximum)).max(axis=-1, keepdims=True)
        p = jnp.exp(sc - m[h])
        lane_sum[h] = _fold_lanes(p, jnp.add)
        acc[h] = _dot(p.astype(BF16), vc_ref[:, heads[h]])

    def pass2_block(h, kb):
        p = jnp.exp(s_ref[h, kb] - m[h])
        lane_sum[h] = lane_sum[h] + _fold_lanes(p, jnp.add)
        acc[h] = acc[h] + _dot(p.astype(BF16), v_ref[kb * tk:(kb + 1) * tk, heads[h]])

    def pass2_finish(h):
        on = acc[h] / lane_sum[h].sum(axis=-1, keepdims=True)
        o = on[:tq] - lam * on[tq:]
        o_ref[:, heads[h]] = _subln(o, g_ref[...], lam_init).astype(o_ref.dtype)

    for t in range(nh + 1):
        for kb in range(nkb):
            if t < nh:
                pass1_block(t, kb)
            if t >= 1:
                pass2_block(t - 1, kb)
        if t < nh:
            pass1_finish(t)
        if t >= 1:
            pass2_finish(t - 1)


def _diff_attention(qkv, qkv_c, lamp, subln_g, lam_init, B, S, Lc):
    tq = 256
    nq = S // tq
    hw = 4 * LANES
    ngrp = MIX_W // hw
    qb, kb, vb = _QB * LANES // hw, _KB * LANES // hw, _VB * LANES // hw
    tk = 512
    nh = hw // LANES
    return pl.pallas_call(
        functools.partial(_diff_kernel, lam_init=lam_init, tk=tk),
        grid=(B, ngrp, nq),
        in_specs=[
            pl.BlockSpec((tq, hw), lambda b, h, i: (b * nq + i, qb + h)),
            pl.BlockSpec((S, hw), lambda b, h, i: (b, kb + h)),
            pl.BlockSpec((S, hw), lambda b, h, i: (b, vb + h)),
            pl.BlockSpec((Lc, hw), lambda b, h, i: (b, kb + h)),
            pl.BlockSpec((Lc, hw), lambda b, h, i: (b, vb + h)),
            pl.BlockSpec((4, HEAD_DIM), lambda b, h, i: (0, 0)),
            pl.BlockSpec((1, LANES), lambda b, h, i: (0, 0)),
        ],
        out_specs=pl.BlockSpec((tq, hw), lambda b, h, i: (b * nq + i, h)),
        out_shape=jax.ShapeDtypeStruct((B * S, MIX_W), BF16),
        scratch_shapes=[pltpu.VMEM((nh, S // tk, 2 * tq, tk), F32)],
        compiler_params=_cparams(("parallel", "parallel", "arbitrary")),
        name="diff_attention",
    )(qkv, qkv, qkv, qkv_c, qkv_c, lamp, subln_g)


def _stack_group(q4):
    return jnp.concatenate([_stack_halves(q4[:, :LANES]), _stack_halves(q4[:, LANES:])], axis=0)


def _merge_group(o, t):
    return jnp.concatenate([_merge_halves(o[:2 * t], t), _merge_halves(o[2 * t:], t)], axis=-1)


def _sink_column(sink_ref, c, t):
    group = sink_ref.shape[0] // SWA_KV_HEADS
    return jnp.concatenate([jnp.full((t, 1), sink_ref[c * group + g], F32) for g in range(group)], axis=0)


def _swa_kernel(sink_ref, q_ref, k_ref, v_ref, kc_ref, vc_ref, mask_ref, o_ref, *, qb):
    c = pl.program_id(1)
    n0 = pl.program_id(2) * qb
    W = SWA_WINDOW
    S = k_ref.shape[0]
    nb = S // W
    kc = kc_ref[...]
    vc = vc_ref[...]
    sink_col = _sink_column(sink_ref, c, W)
    group = sink_ref.shape[0] // SWA_KV_HEADS
    scores = []
    for t in range(qb):
        n = n0 + t
        ks = pl.multiple_of(jnp.clip((n - 1) * W, 0, S - 3 * W), W)
        krows = pl.ds(ks, 3 * W)
        variant = jnp.where(n == 0, 0, jnp.where(n == nb - 1, 2, 1))
        mask = mask_ref[variant]
        qs = _stack_group(q_ref[t * W:(t + 1) * W, :])
        s = _dot_nt(qs, k_ref[krows, :]) + jnp.concatenate([mask] * group, axis=0)
        sc = _dot_nt(qs, kc)
        scores.append((s, sc, krows))
    for t, (s, sc, krows) in enumerate(scores):
        o = _softmax_pv([(s, v_ref[krows, :]), (sc, vc)], extra_logit=sink_col)
        o_ref[t * W:(t + 1) * W, :] = _merge_group(o, W).astype(o_ref.dtype)


def _swa_mask_table():
    W = SWA_WINDOW
    i = np.arange(W)[:, None]
    j = np.arange(3 * W)[None, :]
    tabs = [np.where(np.abs(j - shift - i) <= W, 0.0, MASK_VALUE) for shift in (0, W, 2 * W)]
    return jnp.asarray(np.stack(tabs), F32)


def _swa_attention(qkv, qkv_c, sink, mask, B, S, Lc):
    W = SWA_WINDOW
    nb = S // W
    qb = 8 if nb % 8 == 0 else 1
    nqb = nb // qb
    return pl.pallas_call(
        functools.partial(_swa_kernel, qb=qb),
        grid=(B, SWA_KV_HEADS, nqb),
        in_specs=[
            pl.BlockSpec(memory_space=pltpu.SMEM),
            pl.BlockSpec((qb * W, 2 * LANES), lambda b, c, n: (b * nqb + n, _QC // 2 + c)),
            pl.BlockSpec((S, LANES), lambda b, c, n: (b, _KC + c)),
            pl.BlockSpec((S, LANES), lambda b, c, n: (b, _VC + c)),
            pl.BlockSpec((Lc, LANES), lambda b, c, n: (b, _KC + c)),
            pl.BlockSpec((Lc, LANES), lambda b, c, n: (b, _VC + c)),
            pl.BlockSpec(mask.shape, lambda b, c, n: (0, 0, 0)),
        ],
        out_specs=pl.BlockSpec((qb * W, 2 * LANES), lambda b, c, n: (b * nqb + n, c)),
        out_shape=jax.ShapeDtypeStruct((B * S, MIX_W), BF16),
        compiler_params=_cparams(("parallel", "parallel", "arbitrary")),
        name="swa_attention",
    )(sink, qkv, qkv, qkv, qkv_c, qkv_c, mask)


def _ctx_attn_kernel(sink_ref, qkv_ref, lamp_ref, g_ref, oa_ref, ob_ref, os_ref, *, lam_init):
    Lc = qkv_ref.shape[0]

    def cols(blk, width=LANES):
        return qkv_ref[:, blk * LANES: blk * LANES + width]

    for p in range(MIX_W // LANES):
        qs = _stack_halves(cols(_QA + p))
        o = _softmax_pv([(_dot_nt(qs, cols(_KA + p)), cols(_VA + p))])
        oa_ref[:, p * LANES:(p + 1) * LANES] = _merge_halves(o, Lc).astype(oa_ref.dtype)

    lam = _lambda_full(lamp_ref, lam_init)
    for h in range(MIX_W // LANES):
        qs = _stack_halves(cols(_QB + h))
        on = _softmax_pv([(_dot_nt(qs, cols(_KB + h)), cols(_VB + h))])
        o = on[:Lc] - lam * on[Lc:]
        ob_ref[:, h * LANES:(h + 1) * LANES] = _subln(o, g_ref[...], lam_init).astype(ob_ref.dtype)

    for c in range(SWA_KV_HEADS):
        qs = _stack_group(cols(_QC + 2 * c, 2 * LANES))
        o = _softmax_pv([(_dot_nt(qs, cols(_KC + c)), cols(_VC + c))],
                        extra_logit=_sink_column(sink_ref, c, Lc))
        os_ref[:, c * 2 * LANES:(c + 1) * 2 * LANES] = _merge_group(o, Lc).astype(os_ref.dtype)


def _ctx_attention(qkv_c, sink, lamp, subln_g, lam_init, B, Lc):
    NQ = qkv_c.shape[1]
    out = jax.ShapeDtypeStruct((B * Lc, MIX_W), BF16)
    ospec = pl.BlockSpec((Lc, MIX_W), lambda b: (b, 0))
    return pl.pallas_call(
        functools.partial(_ctx_attn_kernel, lam_init=lam_init),
        grid=(B,),
        in_specs=[
            pl.BlockSpec(memory_space=pltpu.SMEM),
            pl.BlockSpec((Lc, NQ), lambda b: (b, 0)),
            pl.BlockSpec((4, HEAD_DIM), lambda b: (0, 0)),
            pl.BlockSpec((1, LANES), lambda b: (0, 0)),
        ],
        out_specs=[ospec, ospec, ospec],
        out_shape=[out, out, out],
        compiler_params=_cparams(("parallel",)),
        name="ctx_attention",
    )(sink, qkv_c, lamp, subln_g)


def _shift_down(x, k, t_idx):
    return jnp.where(t_idx >= k, pltpu.roll(x, k, 0), 0.0)


def _shift_up(x, k, t_idx):
    T = x.shape[0]
    return jnp.where(t_idx < T - k, pltpu.roll(x, T - k, 0), 0.0)


def _pool_kernel(u_ref, w_ref, scale_ref, o_ref):
    T = u_ref.shape[0]
    pg = u_ref.shape[1] // len(POOL_WINDOWS)
    t_idx = lax.broadcasted_iota(jnp.int32, (T, pg), 0)
    for g, w in enumerate(POOL_WINDOWS):
        u = u_ref[:, g * pg:(g + 1) * pg]
        back = u
        fwd = u
        span = 1
        while span < w // 2:
            back = back + _shift_down(back, span, t_idx)
            fwd = fwd + _shift_up(fwd, span, t_idx)
            span *= 2
        win = _shift_down(back, 1, t_idx) + fwd
        lo = jnp.clip(t_idx - w // 2, 0, T)
        hi = jnp.clip(t_idx - w // 2 + w, 0, T)
        d = (win / (hi - lo).astype(F32) - u).astype(BF16)
        mixed = _dot(d, w_ref[g])
        o_ref[:, g * pg:(g + 1) * pg] = (mixed * scale_ref[:, g * pg:(g + 1) * pg]).astype(o_ref.dtype)


def _pool_mixer(du, w_grp, scale, nseq, T):
    return pl.pallas_call(
        _pool_kernel,
        grid=(nseq,),
        in_specs=[
            pl.BlockSpec((T, MIX_W), lambda b: (b, 0)),
            pl.BlockSpec(w_grp.shape, lambda b: (0, 0, 0)),
            pl.BlockSpec((1, MIX_W), lambda b: (0, 0)),
        ],
        out_specs=pl.BlockSpec((T, MIX_W), lambda b: (b, 0)),
        out_shape=jax.ShapeDtypeStruct((nseq * T, MIX_W), BF16),
        compiler_params=_cparams(("parallel",)),
        name="pool_mixer",
    )(du, w_grp, scale)


def _merge_kernel(h_ref, oa_ref, ob_ref, os_ref, od_ref, wg_ref, wb_ref, acc_ref, f32_ref):
    tn = acc_ref.shape[1]
    h = h_ref[...]
    for n, o_ref in enumerate((oa_ref, ob_ref, os_ref, od_ref)):
        gate = jax.nn.sigmoid(_dot(h, wg_ref[:, n * tn:(n + 1) * tn]))
        term = gate * _dot(o_ref[...], wb_ref[n])
        if n == 0:
            f32_ref[...] = term
        else:
            f32_ref[...] += term
    acc_ref[...] = f32_ref[...].astype(acc_ref.dtype)


def _merge(h, outs, w_gate_r, w_br, tm, tn):
    M, D = h.shape
    ospec = pl.BlockSpec((tm, MIX_W), lambda i, j: (i, 0))
    return pl.pallas_call(
        _merge_kernel,
        grid=(M // tm, D // tn),
        in_specs=[
            pl.BlockSpec((tm, D), lambda i, j: (i, 0)),
            ospec, ospec, ospec, ospec,
            pl.BlockSpec((D, N_BRANCH * tn), lambda i, j: (0, j)),
            pl.BlockSpec((N_BRANCH, MIX_W, tn), lambda i, j: (0, 0, j)),
        ],
        out_specs=pl.BlockSpec((tm, tn), lambda i, j: (i, j)),
        out_shape=jax.ShapeDtypeStruct((M, D), BF16),
        scratch_shapes=[pltpu.VMEM((tm, tn), F32)],
        compiler_params=_cparams(("parallel", "arbitrary")),
        name="gated_merge",
    )(h, *outs, w_gate_r, w_br)


def _outproj_kernel(a_ref, w_ref, x_ref, g_ref, o_ref, *, tc):
    a = a_ref[...]
    for c in range(o_ref.shape[1] // tc):
        cs = slice(c * tc, (c + 1) * tc)
        o_ref[:, cs] = x_ref[:, cs] + g_ref[:, cs] * _dot(a, w_ref[:, cs])


def _out_project(acc, w_o, x2, mod4, row_of_tile, tm):
    M, D = x2.shape
    return pl.pallas_call(
        functools.partial(_outproj_kernel, tc=512),
        grid=(M // tm,),
        in_specs=[
            pl.BlockSpec((tm, D), lambda i: (i, 0)),
            pl.BlockSpec((D, D), lambda i: (0, 0), pipeline_mode=pl.Buffered(1)),
            pl.BlockSpec((tm, D), lambda i: (i, 0)),
            pl.BlockSpec((None, None, 1, D), lambda i: (row_of_tile(i), 2, 0, 0)),
        ],
        out_specs=pl.BlockSpec((tm, D), lambda i: (i, 0)),
        out_shape=jax.ShapeDtypeStruct((M, D), F32),
        compiler_params=_cparams(("parallel",)),
        name="out_proj",
    )(acc, w_o, x2, mod4)


def _ffn_kernel(x_ref, g_ref, sh_ref, sc_ref, gate_ref, wg_ref, wu_ref, wd_ref, o_ref, h_ref, *, tc):
    f = pl.program_id(1)
    D = o_ref.shape[1]

    @pl.when(f == 0)
    def _():
        _modnorm_into(h_ref, x_ref, g_ref, sh_ref, sc_ref)
        o_ref[...] = jnp.zeros_like(o_ref)

    h = h_ref[...]
    a = _dot(h, wg_ref[...])
    u = _dot(h, wu_ref[...])
    act = (a * jax.nn.sigmoid(a) * u).astype(BF16)
    for c in range(D // tc):
        cs = slice(c * tc, (c + 1) * tc)
        o_ref[:, cs] += _dot(act, wd_ref[:, cs])

    @pl.when(f == pl.num_programs(1) - 1)
    def _():
        o_ref[...] = x_ref[...] + gate_ref[...] * o_ref[...]


def _ffn(x2, mod4, row_of_tile, g_norm, wg, wu, wd, tm, tf):
    M, D = x2.shape
    F = wg.shape[1]
    mod_spec = lambda k: pl.BlockSpec((None, None, 1, D), lambda i, f: (row_of_tile(i), k, 0, 0))
    return pl.pallas_call(
        functools.partial(_ffn_kernel, tc=512),
        grid=(M // tm, F // tf),
        in_specs=[
            pl.BlockSpec((tm, D), lambda i, f: (i, 0)),
            pl.BlockSpec((1, D), lambda i, f: (0, 0)),
            mod_spec(3), mod_spec(4), mod_spec(5),
            pl.BlockSpec((D, tf), lambda i, f: (0, f)),
            pl.BlockSpec((D, tf), lambda i, f: (0, f)),
            pl.BlockSpec((tf, D), lambda i, f: (f, 0)),
        ],
        out_specs=pl.BlockSpec((tm, D), lambda i, f: (i, 0)),
        out_shape=jax.ShapeDtypeStruct((M, D), F32),
        scratch_shapes=[pltpu.VMEM((tm, D), BF16)],
        compiler_params=_cparams(("parallel", "arbitrary")),
        name="swiglu_ffn",
    )(x2, g_norm, mod4, mod4, mod4, wg, wu, wd)


def _rope_tables(S):
    t = jnp.arange(S)
    pos = jnp.stack([t // GRID_W, t % GRID_W], axis=-1).astype(F32)
    n_freq = HEAD_DIM // 4
    inv = ROPE_BASE ** (-jnp.arange(n_freq, dtype=F32) / n_freq)
    ang = pos[:, :, None] * inv
    cos, sin = jnp.cos(ang), jnp.sin(ang)
    cos_h = jnp.concatenate([cos, cos], axis=-1).reshape(S, HEAD_DIM)
    sin_h = jnp.concatenate([-sin, sin], axis=-1).reshape(S, HEAD_DIM)
    reps = MIX_W // HEAD_DIM
    return jnp.tile(cos_h, (1, reps)), jnp.tile(sin_h, (1, reps))


def _split_w_in(w_in_l):
    sizes = (MIX_W,) * 7 + (SWA_KV_HEADS * HEAD_DIM,) * 2 + (MIX_W,)
    offs = [0]
    for s in sizes:
        offs.append(offs[-1] + s)
    parts = [w_in_l[:, offs[i]:offs[i + 1]] for i in range(len(sizes))]
    return parts, w_in_l[:, offs[-1]:]


def _dup_heads(w):
    D = w.shape[0]
    return jnp.repeat(w.reshape(D, SWA_KV_HEADS, 1, HEAD_DIM), 2, axis=2).reshape(D, 2 * SWA_KV_HEADS * HEAD_DIM)


def _tile_heads(g, scale=1.0):
    return jnp.tile(g.astype(F32) * scale, MIX_W // HEAD_DIM)


def _proj_column_vectors(gains):
    rope_blocks = (3, 4, 6, 7)
    half = MIX_W // 2
    rows = []
    for j, g in enumerate(gains):
        gain = jnp.ones((MIX_W,), F32) if g is None else g
        normed = jnp.zeros((MIX_W,), F32) if g is None else jnp.ones((MIX_W,), F32)
        roped = jnp.ones((MIX_W,), F32) if j in rope_blocks else jnp.zeros((MIX_W,), F32)
        if j == 7:
            keep = (jnp.arange(MIX_W) < half).astype(F32)
            gain = jnp.where(keep > 0, gain, 1.0)
            normed, roped = normed * keep, roped * keep
        rows.append(jnp.stack([gain, normed, roped] + [jnp.zeros((MIX_W,), F32)] * 5))
    return jnp.stack(rows)


def _pick_tile(M, unit, target):
    t = min(target, M)
    while M % t or (t > unit and t % unit) or (t < unit and unit % t):
        t //= 2
    return t


def kernel(x, c, ctx, c_ctx, w_ada, b_ada, norm_mix, norm_ffn, w_in, a_q_norm, a_k_norm, a_rpb, b_q_norm, b_k_norm, b_lam_q1, b_lam_k1, b_lam_q2, b_lam_k2, b_subln, c_q_norm, c_k_norm, c_sink, d_w, d_scale, w_branch, w_out, w_ffn_gate, w_ffn_up, w_ffn_down):
    B, S, D = x.shape
    Lc = ctx.shape[1]
    L = w_ada.shape[0]
    assert D == N_BRANCH * MIX_W and S % GRID_W == 0 and S // GRID_W >= NA_ROWS and S >= 3 * SWA_WINDOW

    tm = _pick_tile(S, S, 512)
    tm_proj = _pick_tile(S, S, 1024)
    tm_ffn = _pick_tile(S, S, 1024)
    lat_row_ffn = lambda i: i // (S // tm_ffn)
    tm_merge = _pick_tile(B * S, 8, 1024)
    tmc = _pick_tile(B * Lc, 8, 512)
    lat_row = lambda i: i // (S // tm)
    lat_row_proj = lambda i: i // (S // tm_proj)
    ctx_row = lambda i: B
    n_mod_rows = -(-(B + 1) // 8) * 8

    cc = jnp.zeros((n_mod_rows, D), F32).at[:B].set(c).at[B].set(c_ctx)
    mod = _modulation(cc, w_ada, b_ada).reshape(L, n_mod_rows, 6, 1, D)

    cos_t, sin_t = _rope_tables(S)
    cos_id = jnp.ones((tmc, MIX_W), F32)
    sin_id = jnp.zeros((tmc, MIX_W), F32)
    lane = jnp.arange(2 * LANES)
    bd = (lane[:, None] // HEAD_DIM == lane[None, :] // HEAD_DIM).astype(BF16)
    swa_mask = _swa_mask_table()

    x2 = x.reshape(B * S, D)
    c2 = ctx.reshape(B * Lc, D)
    tn_merge = 256

    for l in range(L):
        last = l == L - 1
        lam_init = 0.8 - 0.6 * math.exp(-0.3 * l)
        mod_l = mod[l]

        (aq, ak, av, bq, bk, bv, sq, sk, sv, du), w_gl = _split_w_in(w_in[l])
        w_cat = jnp.concatenate([aq, ak, av, bq, bk, bv, sq, _dup_heads(sk), _dup_heads(sv), du], axis=1).astype(BF16)
        pvecs = _proj_column_vectors([
            _tile_heads(a_q_norm[l], ATTN_SCALE), _tile_heads(a_k_norm[l]), None,
            _tile_heads(b_q_norm[l], ATTN_SCALE), _tile_heads(b_k_norm[l]), None,
            _tile_heads(c_q_norm[l], ATTN_SCALE),
            _tile_heads(c_k_norm[l]),
        ])
        w_gate_r = (w_gl.reshape(D, N_BRANCH, D // tn_merge, tn_merge).transpose(0, 2, 1, 3)
                    .reshape(D, N_BRANCH * D).astype(BF16))
        w_br = w_branch[l].astype(BF16)
        w_o = w_out[l].astype(BF16)
        wg, wu, wd = w_ffn_gate[l].astype(BF16), w_ffn_up[l].astype(BF16), w_ffn_down[l].astype(BF16)
        g_mix_norm = norm_mix[l].reshape(1, D)
        g_ffn_norm = norm_ffn[l].reshape(1, D)
        lamp = jnp.stack([b_lam_q1[l], b_lam_k1[l], b_lam_q2[l], b_lam_k2[l]]).astype(F32)
        subln_g = b_subln[l].reshape(1, LANES).astype(F32)
        sink = c_sink[l].astype(F32)
        w_pool = d_w[l].astype(BF16)
        pool_scale = d_scale[l].reshape(1, MIX_W).astype(F32)

        qkv, du_l, h_l = _project(x2, mod_l, lat_row_proj, g_mix_norm, w_cat, pvecs, cos_t, sin_t,
                                  lambda i: i % (S // tm_proj), bd, tm_proj)
        qkv_c, du_c, h_c = _project(c2, mod_l, ctx_row, g_mix_norm, w_cat, pvecs, cos_id, sin_id,
                                    lambda i: 0, bd, tmc)

        o_a = _na_attention(qkv, qkv_c, _na_bias_table(a_rpb[l], S // GRID_W), B, S, Lc)
        o_b = _diff_attention(qkv, qkv_c, lamp, subln_g, lam_init, B, S, Lc)
        o_s = _swa_attention(qkv, qkv_c, sink, swa_mask, B, S, Lc)
        o_d = _pool_mixer(du_l, w_pool, pool_scale, B, S)
        acc = _merge(h_l, (o_a, o_b, o_s, o_d), w_gate_r, w_br, tm_merge, tn_merge)
        x2 = _out_project(acc, w_o, x2, mod_l, lat_row, tm)

        if not last:
            oc_a, oc_b, oc_s = _ctx_attention(qkv_c, sink, lamp, subln_g, lam_init, B, Lc)
            oc_d = _pool_mixer(du_c, w_pool, pool_scale, B, Lc)
            acc_c = _merge(h_c, (oc_a, oc_b, oc_s, oc_d), w_gate_r, w_br, tmc, tn_merge)
            c2 = _out_project(acc_c, w_o, c2, mod_l, ctx_row, tmc)
            c2 = _ffn(c2, mod_l, ctx_row, g_ffn_norm, wg, wu, wd, tmc, 512)

        x2 = _ffn(x2, mod_l, lat_row_ffn, g_ffn_norm, wg, wu, wd, tm_ffn, 512)

    return x2.reshape(B, S, D)
```

```python
import functools
import math

import numpy as np
import jax
import jax.numpy as jnp
from jax import lax
from jax.experimental import pallas as pl
from jax.experimental.pallas import tpu as pltpu

F32 = jnp.float32
BF16 = jnp.bfloat16

HEAD_DIM = 64
LANES = 128
GRID_W = 64
N_BRANCH = 4
NA_ROWS = 8
NA_COLS = 16
SWA_WINDOW = 128
SWA_KV_HEADS = 2
POOL_WINDOWS = (2, 4, 8, 16)
ROPE_BASE = 10000.0
NORM_EPS = 1e-6
MASK_VALUE = -1e30
ATTN_SCALE = HEAD_DIM ** -0.5
MIX_W = 512
VMEM_LIMIT = 56 * 1024 * 1024


def _cparams(sem):
    return pltpu.CompilerParams(dimension_semantics=sem, vmem_limit_bytes=VMEM_LIMIT)


def _dot(a, b):
    return jnp.dot(a, b, preferred_element_type=F32)


def _dot_nt(a, b):
    return lax.dot_general(a, b, (((1,), (1,)), ((), ())), preferred_element_type=F32)


def _lane_lo(shape):
    return lax.broadcasted_iota(jnp.int32, shape, len(shape) - 1) % LANES < HEAD_DIM


def _stack_halves(q2):
    lo = _lane_lo(q2.shape)
    zero = jnp.zeros_like(q2)
    return jnp.concatenate([jnp.where(lo, q2, zero), jnp.where(lo, zero, q2)], axis=0)


def _merge_halves(o, t):
    lo = _lane_lo((t, LANES))
    return jnp.where(lo, o[:t], o[t:])


def _mod_kernel(c_ref, w_ref, b_ref, o_ref):
    c = c_ref[...]
    act = (c * jax.nn.sigmoid(c)).astype(BF16)
    o_ref[...] = _dot(act, w_ref[...].astype(BF16)) + b_ref[...]


def _modulation(cc, w_ada, b_ada):
    L, D, N = w_ada.shape
    R = cc.shape[0]
    tn = 1024
    return pl.pallas_call(
        _mod_kernel,
        grid=(L, N // tn),
        in_specs=[
            pl.BlockSpec((R, D), lambda l, j: (0, 0)),
            pl.BlockSpec((None, D, tn), lambda l, j: (l, 0, j)),
            pl.BlockSpec((None, 1, tn), lambda l, j: (l, 0, j)),
        ],
        out_specs=pl.BlockSpec((None, R, tn), lambda l, j: (l, 0, j)),
        out_shape=jax.ShapeDtypeStruct((L, R, N), F32),
        compiler_params=_cparams(("parallel", "parallel")),
        name="adaln_mod",
    )(cc, w_ada, b_ada.reshape(L, 1, N))


def _modnorm_into(dst_ref, x_ref, g_ref, sh_ref, sc_ref, chunk=128):
    tm = x_ref.shape[0]
    sh = sh_ref[...]
    gain = g_ref[...] * (1.0 + sc_ref[...])

    def body(r, carry):
        rows = pl.ds(pl.multiple_of(r * chunk, chunk), chunk)
        xf = x_ref[rows, :]
        ms = jnp.mean(xf * xf, axis=-1, keepdims=True)
        dst_ref[rows, :] = (xf * lax.rsqrt(ms + NORM_EPS) * gain + sh).astype(dst_ref.dtype)
        return carry

    lax.fori_loop(0, tm // chunk, body, 0)


def _proj_kernel(x_ref, g_ref, sh_ref, sc_ref, w_ref, pv_ref, cos_ref, sin_ref, bd_ref,
                 qkv_ref, du_ref, h_ref, y_ref, *, plain_steps):
    j = pl.program_id(1)

    @pl.when(j == 0)
    def _():
        _modnorm_into(h_ref, x_ref, g_ref, sh_ref, sc_ref)
        y_ref[...] = jnp.zeros_like(y_ref)

    def norm_rope(yp):
        gain = pv_ref[0:1, :]
        normed = pv_ref[1:2, :] > 0.0
        roped = pv_ref[2:3, :] > 0.0
        y2 = (yp * yp).astype(BF16)
        hw = bd_ref.shape[0]
        ss = jnp.concatenate([_dot(y2[:, c * hw:(c + 1) * hw], bd_ref[...]) for c in range(y2.shape[1] // hw)],
                             axis=1)
        rs = lax.rsqrt(ss * (1.0 / HEAD_DIM) + NORM_EPS)
        n = yp * (jnp.where(normed, rs, 1.0) * gain)
        cos = jnp.where(roped, cos_ref[...], 1.0)
        sin = jnp.where(roped, sin_ref[...], 0.0)
        w = n.shape[-1]
        lane = lax.broadcasted_iota(jnp.int32, n.shape, 1)
        partner = jnp.where((lane % 32) < 16, pltpu.roll(n, w - 16, 1), pltpu.roll(n, 16, 1))
        return n * cos + partner * sin

    def step(epilogue):
        qkv_ref[...] = epilogue(y_ref[...]).astype(BF16)
        y = _dot(h_ref[...], w_ref[...])
        y_ref[...] = y

        @pl.when(j == pl.num_programs(1) - 1)
        def _():
            du_ref[...] = y

    prev_plain = functools.reduce(lambda a, b: a | b, [j == jp for jp in plain_steps])

    @pl.when(prev_plain)
    def _():
        step(lambda yp: yp)

    @pl.when(jnp.logical_not(prev_plain))
    def _():
        step(norm_rope)


_PLAIN_QKV_BLOCKS = (2, 5)


def _project(x2, mod5, l, row_of_tile, g_norm, w_cat, pvecs, cos_t, sin_t, table_block, bd, tm):
    M, D = x2.shape
    nblk = w_cat.shape[2] // MIX_W
    nq = nblk - 1
    mod_spec = lambda k: pl.BlockSpec((None, None, None, 1, D), lambda i, j: (l, row_of_tile(i), k, 0, 0))
    plain_steps = (0,) + tuple(b + 1 for b in _PLAIN_QKV_BLOCKS)
    return pl.pallas_call(
        functools.partial(_proj_kernel, plain_steps=plain_steps),
        grid=(M // tm, nblk),
        in_specs=[
            pl.BlockSpec((tm, D), lambda i, j: (i, 0)),
            pl.BlockSpec((1, D), lambda i, j: (0, 0)),
            mod_spec(0), mod_spec(1),
            pl.BlockSpec((None, D, MIX_W), lambda i, j: (l, 0, j)),
            pl.BlockSpec((None, 8, MIX_W), lambda i, j: (jnp.maximum(j - 1, 0), 0, 0)),
            pl.BlockSpec((tm, MIX_W), lambda i, j: (table_block(i), 0)),
            pl.BlockSpec((tm, MIX_W), lambda i, j: (table_block(i), 0)),
            pl.BlockSpec(bd.shape, lambda i, j: (0, 0)),
        ],
        out_specs=[
            pl.BlockSpec((tm, MIX_W), lambda i, j: (i, jnp.maximum(j - 1, 0))),
            pl.BlockSpec((tm, MIX_W), lambda i, j: (i, 0)),
            pl.BlockSpec((tm, D), lambda i, j: (i, 0)),
        ],
        out_shape=[
            jax.ShapeDtypeStruct((M, nq * MIX_W), BF16),
            jax.ShapeDtypeStruct((M, MIX_W), F32),
            jax.ShapeDtypeStruct((M, D), BF16),
        ],
        scratch_shapes=[pltpu.VMEM((tm, MIX_W), F32)],
        compiler_params=_cparams(("parallel", "arbitrary")),
        name="in_proj",
    )(x2, g_norm, mod5, mod5, w_cat, pvecs, cos_t, sin_t, bd)


_QA, _KA, _VA = 0, 4, 8
_QB, _KB, _VB = 12, 16, 20
_QC = 24
_KC, _VC = 28, 30


def _softmax_pv(parts, extra_logit=None):
    s = jnp.concatenate([s for s, _ in parts], axis=1) if len(parts) > 1 else parts[0][0]
    m = s.max(axis=-1, keepdims=True)
    if extra_logit is not None:
        m = jnp.maximum(m, extra_logit)
    p = jnp.exp(s - m)
    l = p.sum(axis=-1, keepdims=True)
    if extra_logit is not None:
        l = l + jnp.exp(extra_logit - m)
    pb = p.astype(BF16)
    o = None
    start = 0
    for sp, v in parts:
        width = sp.shape[1]
        term = _dot(pb[:, start:start + width], v)
        o = term if o is None else o + term
        start += width
    return o / l


NA_GROUP = 4
NA_UNION = NA_GROUP + NA_ROWS


def _na_kernel(q_ref, k_ref, v_ref, kc_ref, vc_ref, bias_ref, o_ref, *, rows, units):
    g0 = pl.program_id(2) * units
    n_groups = rows // NA_GROUP
    tq = NA_GROUP * GRID_W
    kc = kc_ref[...]
    vc = vc_ref[...]
    scores = []
    for u in range(units):
        g = g0 + u
        start = jnp.clip(NA_GROUP * g - NA_ROWS // 2, 0, rows - NA_UNION)
        variant = jnp.where(g == 0, 0, jnp.where(g == n_groups - 1, 2, 1))
        krows = pl.ds(pl.multiple_of(start * GRID_W, GRID_W), NA_UNION * GRID_W)
        qs = _stack_halves(q_ref[u * tq:(u + 1) * tq, :])
        s = _dot_nt(qs, k_ref[krows, :]) + bias_ref[variant]
        sc = _dot_nt(qs, kc)
        scores.append((s, sc, krows))
    for u, (s, sc, krows) in enumerate(scores):
        o = _softmax_pv([(s, v_ref[krows, :]), (sc, vc)])
        o_ref[u * tq:(u + 1) * tq, :] = _merge_halves(o, tq).astype(o_ref.dtype)


def _na_attention(qkv, qkv_c, bias, B, S, Lc):
    rows = S // GRID_W
    assert rows % NA_GROUP == 0 and rows >= NA_UNION
    n_groups = rows // NA_GROUP
    units = 4 if n_groups % 4 == 0 else 1
    nstep = n_groups // units
    tq = units * NA_GROUP * GRID_W
    npair = MIX_W // LANES
    return pl.pallas_call(
        functools.partial(_na_kernel, rows=rows, units=units),
        grid=(B, npair, nstep),
        in_specs=[
            pl.BlockSpec((tq, LANES), lambda b, p, r: (b * nstep + r, _QA + p)),
            pl.BlockSpec((S, LANES), lambda b, p, r: (b, _KA + p)),
            pl.BlockSpec((S, LANES), lambda b, p, r: (b, _VA + p)),
            pl.BlockSpec((Lc, LANES), lambda b, p, r: (b, _KA + p)),
            pl.BlockSpec((Lc, LANES), lambda b, p, r: (b, _VA + p)),
            pl.BlockSpec((None,) + bias.shape[1:], lambda b, p, r: (p, 0, 0, 0)),
        ],
        out_specs=pl.BlockSpec((tq, LANES), lambda b, p, r: (b * nstep + r, p)),
        out_shape=jax.ShapeDtypeStruct((B * S, MIX_W), BF16),
        compiler_params=_cparams(("parallel", "parallel", "arbitrary")),
        name="na_attention",
    )(qkv, qkv, qkv, qkv_c, qkv_c, bias)


def _na_bias_kernel(rpb_ref, onehot_ref, valid_ref, o_ref):
    r = rpb_ref[...]
    hi = r.astype(BF16)
    r1 = r - hi.astype(F32)
    mid = r1.astype(BF16)
    lo = (r1 - mid.astype(F32)).astype(BF16)
    oh = onehot_ref[...]
    t = _dot(hi, oh) + _dot(mid, oh) + _dot(lo, oh)
    o_ref[...] = jnp.where(valid_ref[...] > 0.0, t, MASK_VALUE)


def _na_window_pattern(rows, g):
    start_u = np.clip(NA_GROUP * g - NA_ROWS // 2, 0, rows - NA_UNION)
    r = NA_GROUP * g + np.arange(NA_GROUP)[:, None]
    start_r = np.clip(r - NA_ROWS // 2, 0, rows - NA_ROWS)
    kr = start_u + np.arange(NA_UNION)[None, :]
    valid = (kr >= start_r) & (kr < start_r + NA_ROWS)
    return np.where(valid, kr - r + (NA_ROWS - 1), 0), valid


def _na_bias_table(rpb, rows):
    H, n_dr, n_dc = rpb.shape
    n_groups = rows // NA_GROUP
    patterns = [_na_window_pattern(rows, g) for g in (0, 1, n_groups - 1)]
    for g in range(1, n_groups - 1):
        dr_g, valid_g = _na_window_pattern(rows, g)
        assert (dr_g == patterns[1][0]).all() and (valid_g == patterns[1][1]).all()
    col = np.arange(GRID_W)
    col_start = np.clip(col - NA_COLS // 2, 0, GRID_W - NA_COLS)
    valid = (col[None, :] >= col_start[:, None]) & (col[None, :] < col_start[:, None] + NA_COLS)
    dc = np.clip(col[None, :] - col[:, None], -(NA_COLS - 1), NA_COLS - 1) + (NA_COLS - 1)
    onehot = (np.arange(LANES)[:, None] == dc.reshape(1, -1)).astype(np.float32)
    n_rows = -(-(H * n_dr) // LANES) * LANES
    rpb_rows = jnp.zeros((n_rows, LANES), F32).at[:H * n_dr, :n_dc].set(rpb.reshape(H * n_dr, n_dc).astype(F32))
    t = pl.pallas_call(
        _na_bias_kernel,
        out_shape=jax.ShapeDtypeStruct((n_rows, GRID_W * GRID_W), F32),
        name="na_bias",
    )(rpb_rows, jnp.asarray(onehot, BF16), jnp.asarray(valid.reshape(1, -1), F32))
    t3 = t.reshape(n_rows, GRID_W, GRID_W)
    out_rows, out_cols = 2 * NA_GROUP * GRID_W, NA_UNION * GRID_W
    return pl.pallas_call(
        functools.partial(_na_assemble_kernel, patterns=patterns, n_dr=n_dr),
        grid=(H // 2,),
        in_specs=[pl.BlockSpec(t3.shape, lambda p: (0, 0, 0))],
        out_specs=pl.BlockSpec((None, len(patterns), out_rows, out_cols), lambda p: (p, 0, 0, 0)),
        out_shape=jax.ShapeDtypeStruct((H // 2, len(patterns), out_rows, out_cols), F32),
        compiler_params=_cparams(("parallel",)),
        name="na_bias_assemble",
    )(t3)


def _na_assemble_kernel(t3_ref, o_ref, *, patterns, n_dr):
    p = pl.program_id(0)
    masked = jnp.full((GRID_W, GRID_W), MASK_VALUE, F32)
    for v, (dr, ok) in enumerate(patterns):
        for parity in range(2):
            base = (2 * p + parity) * n_dr
            for rr in range(NA_GROUP):
                r0 = (parity * NA_GROUP + rr) * GRID_W
                for j in range(NA_UNION):
                    tile = t3_ref[base + int(dr[rr, j])] if ok[rr, j] else masked
                    o_ref[v, r0:r0 + GRID_W, j * GRID_W:(j + 1) * GRID_W] = tile


def _lambda_full(lamp_ref, lam_init):
    lp = lamp_ref[...]
    a = jnp.sum(lp[0:1] * lp[1:2], axis=-1, keepdims=True)
    b = jnp.sum(lp[2:3] * lp[3:4], axis=-1, keepdims=True)
    return jnp.exp(a) - jnp.exp(b) + lam_init


def _subln(o, g, lam_init):
    ms = jnp.mean(o * o, axis=-1, keepdims=True)
    return o * lax.rsqrt(ms + NORM_EPS) * g * (1.0 - lam_init)


def _fold_lanes(x, op):
    out = x[:, :LANES]
    for b in range(1, x.shape[1] // LANES):
        out = op(out, x[:, b * LANES:(b + 1) * LANES])
    return out


def _diff_kernel(q_ref, k_ref, v_ref, kc_ref, vc_ref, lamp_ref, g_ref, o_ref, s_ref, *, lam_init, tk):
    tq = q_ref.shape[0]
    S = k_ref.shape[0]
    nkb = S // tk
    lam = _lambda_full(lamp_ref, lam_init)
    nh = q_ref.shape[1] // LANES
    heads = [slice(h * LANES, (h + 1) * LANES) for h in range(nh)]
    qs = [_stack_halves(q_ref[:, hs]) for hs in heads]
    lane_max, m, lane_sum, acc = [None] * nh, [None] * nh, [None] * nh, [None] * nh

    def pass1_block(h, kb):
        s = _dot_nt(qs[h], k_ref[kb * tk:(kb + 1) * tk, heads[h]])
        s_ref[h, kb] = s
        blk = _fold_lanes(s, jnp.maximum)
        lane_max[h] = blk if lane_max[h] is None else jnp.maximum(lane_max[h], blk)

    def pass1_finish(h):
        sc = _dot_nt(qs[h], kc_ref[:, heads[h]])
        m[h] = jnp.maximum(lane_max[h], _fold_lanes(sc, jnp.maximum)).max(axis=-1, keepdims=True)
        p = jnp.exp(sc - m[h])
        lane_sum[h] = _fold_lanes(p, jnp.add)
        acc[h] = _dot(p.astype(BF16), vc_ref[:, heads[h]])

    def pass2_block(h, kb):
        p = jnp.exp(s_ref[h, kb] - m[h])
        lane_sum[h] = lane_sum[h] + _fold_lanes(p, jnp.add)
        acc[h] = acc[h] + _dot(p.astype(BF16), v_ref[kb * tk:(kb + 1) * tk, heads[h]])

    def pass2_finish(h):
        on = acc[h] / lane_sum[h].sum(axis=-1, keepdims=True)
        o = on[:tq] - lam * on[tq:]
        o_ref[:, heads[h]] = _subln(o, g_ref[...], lam_init).astype(o_ref.dtype)

    for t in range(nh + 1):
        for kb in range(nkb):
            if t < nh:
                pass1_block(t, kb)
            if t >= 1:
                pass2_block(t - 1, kb)
        if t < nh:
            pass1_finish(t)
        if t >= 1:
            pass2_finish(t - 1)


def _diff_attention(qkv, qkv_c, lamp, subln_g, lam_init, B, S, Lc):
    tq = 256
    nq = S // tq
    hw = 4 * LANES
    ngrp = MIX_W // hw
    qb, kb, vb = _QB * LANES // hw, _KB * LANES // hw, _VB * LANES // hw
    tk = 512
    nh = hw // LANES
    return pl.pallas_call(
        functools.partial(_diff_kernel, lam_init=lam_init, tk=tk),
        grid=(B, ngrp, nq),
        in_specs=[
            pl.BlockSpec((tq, hw), lambda b, h, i: (b * nq + i, qb + h)),
            pl.BlockSpec((S, hw), lambda b, h, i: (b, kb + h)),
            pl.BlockSpec((S, hw), lambda b, h, i: (b, vb + h)),
            pl.BlockSpec((Lc, hw), lambda b, h, i: (b, kb + h)),
            pl.BlockSpec((Lc, hw), lambda b, h, i: (b, vb + h)),
            pl.BlockSpec((4, HEAD_DIM), lambda b, h, i: (0, 0)),
            pl.BlockSpec((1, LANES), lambda b, h, i: (0, 0)),
        ],
        out_specs=pl.BlockSpec((tq, hw), lambda b, h, i: (b * nq + i, h)),
        out_shape=jax.ShapeDtypeStruct((B * S, MIX_W), BF16),
        scratch_shapes=[pltpu.VMEM((nh, S // tk, 2 * tq, tk), F32)],
        compiler_params=_cparams(("parallel", "parallel", "arbitrary")),
        name="diff_attention",
    )(qkv, qkv, qkv, qkv_c, qkv_c, lamp, subln_g)


def _stack_group(q4):
    return jnp.concatenate([_stack_halves(q4[:, :LANES]), _stack_halves(q4[:, LANES:])], axis=0)


def _merge_group(o, t):
    return jnp.concatenate([_merge_halves(o[:2 * t], t), _merge_halves(o[2 * t:], t)], axis=-1)


def _sink_column(sink_ref, c, t):
    group = sink_ref.shape[0] // SWA_KV_HEADS
    return jnp.concatenate([jnp.full((t, 1), sink_ref[c * group + g], F32) for g in range(group)], axis=0)


def _swa_kernel(sink_ref, q_ref, k_ref, v_ref, kc_ref, vc_ref, mask_ref, o_ref, *, qb):
    c = pl.program_id(1)
    n0 = pl.program_id(2) * qb
    W = SWA_WINDOW
    S = k_ref.shape[0]
    nb = S // W
    kc = kc_ref[...]
    vc = vc_ref[...]
    sink_col = _sink_column(sink_ref, c, W)
    group = sink_ref.shape[0] // SWA_KV_HEADS
    scores = []
    for t in range(qb):
        n = n0 + t
        ks = pl.multiple_of(jnp.clip((n - 1) * W, 0, S - 3 * W), W)
        krows = pl.ds(ks, 3 * W)
        variant = jnp.where(n == 0, 0, jnp.where(n == nb - 1, 2, 1))
        mask = mask_ref[variant]
        qs = _stack_group(q_ref[t * W:(t + 1) * W, :])
        s = _dot_nt(qs, k_ref[krows, :]) + jnp.concatenate([mask] * group, axis=0)
        sc = _dot_nt(qs, kc)
        scores.append((s, sc, krows))
    for t, (s, sc, krows) in enumerate(scores):
        o = _softmax_pv([(s, v_ref[krows, :]), (sc, vc)], extra_logit=sink_col)
        o_ref[t * W:(t + 1) * W, :] = _merge_group(o, W).astype(o_ref.dtype)


def _swa_mask_table():
    W = SWA_WINDOW
    i = np.arange(W)[:, None]
    j = np.arange(3 * W)[None, :]
    tabs = [np.where(np.abs(j - shift - i) <= W, 0.0, MASK_VALUE) for shift in (0, W, 2 * W)]
    return jnp.asarray(np.stack(tabs), F32)


def _swa_attention(qkv, qkv_c, sink, mask, B, S, Lc):
    W = SWA_WINDOW
    nb = S // W
    qb = 8 if nb % 8 == 0 else 1
    nqb = nb // qb
    return pl.pallas_call(
        functools.partial(_swa_kernel, qb=qb),
        grid=(B, SWA_KV_HEADS, nqb),
        in_specs=[
            pl.BlockSpec(memory_space=pltpu.SMEM),
            pl.BlockSpec((qb * W, 2 * LANES), lambda b, c, n: (b * nqb + n, _QC // 2 + c)),
            pl.BlockSpec((S, LANES), lambda b, c, n: (b, _KC + c)),
            pl.BlockSpec((S, LANES), lambda b, c, n: (b, _VC + c)),
            pl.BlockSpec((Lc, LANES), lambda b, c, n: (b, _KC + c)),
            pl.BlockSpec((Lc, LANES), lambda b, c, n: (b, _VC + c)),
            pl.BlockSpec(mask.shape, lambda b, c, n: (0, 0, 0)),
        ],
        out_specs=pl.BlockSpec((qb * W, 2 * LANES), lambda b, c, n: (b * nqb + n, c)),
        out_shape=jax.ShapeDtypeStruct((B * S, MIX_W), BF16),
        compiler_params=_cparams(("parallel", "parallel", "arbitrary")),
        name="swa_attention",
    )(sink, qkv, qkv, qkv, qkv_c, qkv_c, mask)


def _ctx_attn_kernel(sink_ref, qkv_ref, lamp_ref, g_ref, oa_ref, ob_ref, os_ref, *, lam_init):
    Lc = qkv_ref.shape[0]

    def cols(blk, width=LANES):
        return qkv_ref[:, blk * LANES: blk * LANES + width]

    for p in range(MIX_W // LANES):
        qs = _stack_halves(cols(_QA + p))
        o = _softmax_pv([(_dot_nt(qs, cols(_KA + p)), cols(_VA + p))])
        oa_ref[:, p * LANES:(p + 1) * LANES] = _merge_halves(o, Lc).astype(oa_ref.dtype)

    lam = _lambda_full(lamp_ref, lam_init)
    for h in range(MIX_W // LANES):
        qs = _stack_halves(cols(_QB + h))
        on = _softmax_pv([(_dot_nt(qs, cols(_KB + h)), cols(_VB + h))])
        o = on[:Lc] - lam * on[Lc:]
        ob_ref[:, h * LANES:(h + 1) * LANES] = _subln(o, g_ref[...], lam_init).astype(ob_ref.dtype)

    for c in range(SWA_KV_HEADS):
        qs = _stack_group(cols(_QC + 2 * c, 2 * LANES))
        o = _softmax_pv([(_dot_nt(qs, cols(_KC + c)), cols(_VC + c))],
                        extra_logit=_sink_column(sink_ref, c, Lc))
        os_ref[:, c * 2 * LANES:(c + 1) * 2 * LANES] = _merge_group(o, Lc).astype(os_ref.dtype)


def _ctx_attention(qkv_c, sink, lamp, subln_g, lam_init, B, Lc):
    NQ = qkv_c.shape[1]
    out = jax.ShapeDtypeStruct((B * Lc, MIX_W), BF16)
    ospec = pl.BlockSpec((Lc, MIX_W), lambda b: (b, 0))
    return pl.pallas_call(
        functools.partial(_ctx_attn_kernel, lam_init=lam_init),
        grid=(B,),
        in_specs=[
            pl.BlockSpec(memory_space=pltpu.SMEM),
            pl.BlockSpec((Lc, NQ), lambda b: (b, 0)),
            pl.BlockSpec((4, HEAD_DIM), lambda b: (0, 0)),
            pl.BlockSpec((1, LANES), lambda b: (0, 0)),
        ],
        out_specs=[ospec, ospec, ospec],
        out_shape=[out, out, out],
        compiler_params=_cparams(("parallel",)),
        name="ctx_attention",
    )(sink, qkv_c, lamp, subln_g)


def _shift_down(x, k, t_idx):
    return jnp.where(t_idx >= k, pltpu.roll(x, k, 0), 0.0)


def _shift_up(x, k, t_idx):
    T = x.shape[0]
    return jnp.where(t_idx < T - k, pltpu.roll(x, T - k, 0), 0.0)


def _pool_kernel(u_ref, w_ref, scale_ref, o_ref):
    T = u_ref.shape[0]
    pg = u_ref.shape[1] // len(POOL_WINDOWS)
    t_idx = lax.broadcasted_iota(jnp.int32, (T, pg), 0)
    for g, w in enumerate(POOL_WINDOWS):
        u = u_ref[:, g * pg:(g + 1) * pg]
        back = u
        fwd = u
        span = 1
        while span < w // 2:
            back = back + _shift_down(back, span, t_idx)
            fwd = fwd + _shift_up(fwd, span, t_idx)
            span *= 2
        win = _shift_down(back, 1, t_idx) + fwd
        lo = jnp.clip(t_idx - w // 2, 0, T)
        hi = jnp.clip(t_idx - w // 2 + w, 0, T)
        d = (win / (hi - lo).astype(F32) - u).astype(BF16)
        mixed = _dot(d, w_ref[g])
        o_ref[:, g * pg:(g + 1) * pg] = (mixed * scale_ref[:, g * pg:(g + 1) * pg]).astype(o_ref.dtype)


def _pool_mixer(du, w_grp, scale, nseq, T):
    return pl.pallas_call(
        _pool_kernel,
        grid=(nseq,),
        in_specs=[
            pl.BlockSpec((T, MIX_W), lambda b: (b, 0)),
            pl.BlockSpec(w_grp.shape, lambda b: (0, 0, 0)),
            pl.BlockSpec((1, MIX_W), lambda b: (0, 0)),
        ],
        out_specs=pl.BlockSpec((T, MIX_W), lambda b: (b, 0)),
        out_shape=jax.ShapeDtypeStruct((nseq * T, MIX_W), BF16),
        compiler_params=_cparams(("parallel",)),
        name="pool_mixer",
    )(du, w_grp, scale)


def _merge_kernel(h_ref, oa_ref, ob_ref, os_ref, od_ref, wg_ref, wb_ref, acc_ref, f32_ref):
    tn = acc_ref.shape[1]
    h = h_ref[...]
    for n, o_ref in enumerate((oa_ref, ob_ref, os_ref, od_ref)):
        gate = jax.nn.sigmoid(_dot(h, wg_ref[:, n * tn:(n + 1) * tn]))
        term = gate * _dot(o_ref[...], wb_ref[n])
        if n == 0:
            f32_ref[...] = term
        else:
            f32_ref[...] += term
    acc_ref[...] = f32_ref[...].astype(acc_ref.dtype)


def _merge(h, outs, l, w_gate_r, w_br, tm, tn):
    M, D = h.shape
    ospec = pl.BlockSpec((tm, MIX_W), lambda i, j: (i, 0))
    return pl.pallas_call(
        _merge_kernel,
        grid=(M // tm, D // tn),
        in_specs=[
            pl.BlockSpec((tm, D), lambda i, j: (i, 0)),
            ospec, ospec, ospec, ospec,
            pl.BlockSpec((None, D, N_BRANCH * tn), lambda i, j: (l, 0, j)),
            pl.BlockSpec((None, N_BRANCH, MIX_W, tn), lambda i, j: (l, 0, 0, j)),
        ],
        out_specs=pl.BlockSpec((tm, tn), lambda i, j: (i, j)),
        out_shape=jax.ShapeDtypeStruct((M, D), BF16),
        scratch_shapes=[pltpu.VMEM((tm, tn), F32)],
        compiler_params=_cparams(("parallel", "arbitrary")),
        name="gated_merge",
    )(h, *outs, w_gate_r, w_br)


def _outproj_kernel(a_ref, w_ref, x_ref, g_ref, o_ref, *, tc):
    a = a_ref[...]
    for c in range(o_ref.shape[1] // tc):
        cs = slice(c * tc, (c + 1) * tc)
        o_ref[:, cs] = x_ref[:, cs] + g_ref[:, cs] * _dot(a, w_ref[:, cs])


def _out_project(acc, l, w_o, x2, mod5, row_of_tile, tm):
    M, D = x2.shape
    return pl.pallas_call(
        functools.partial(_outproj_kernel, tc=512),
        grid=(M // tm,),
        in_specs=[
            pl.BlockSpec((tm, D), lambda i: (i, 0)),
            pl.BlockSpec((None, D, D), lambda i: (l, 0, 0), pipeline_mode=pl.Buffered(1)),
            pl.BlockSpec((tm, D), lambda i: (i, 0)),
            pl.BlockSpec((None, None, None, 1, D), lambda i: (l, row_of_tile(i), 2, 0, 0)),
        ],
        out_specs=pl.BlockSpec((tm, D), lambda i: (i, 0)),
        out_shape=jax.ShapeDtypeStruct((M, D), F32),
        compiler_params=_cparams(("parallel",)),
        name="out_proj",
    )(acc, w_o, x2, mod5)


def _ffn_kernel(x_ref, g_ref, sh_ref, sc_ref, gate_ref, wg_ref, wu_ref, wd_ref, o_ref, h_ref, *, tc):
    f = pl.program_id(1)
    D = o_ref.shape[1]

    @pl.when(f == 0)
    def _():
        _modnorm_into(h_ref, x_ref, g_ref, sh_ref, sc_ref)
        o_ref[...] = jnp.zeros_like(o_ref)

    h = h_ref[...]
    a = _dot(h, wg_ref[...])
    u = _dot(h, wu_ref[...])
    act = (a * jax.nn.sigmoid(a) * u).astype(BF16)
    for c in range(D // tc):
        cs = slice(c * tc, (c + 1) * tc)
        o_ref[:, cs] += _dot(act, wd_ref[:, cs])

    @pl.when(f == pl.num_programs(1) - 1)
    def _():
        o_ref[...] = x_ref[...] + gate_ref[...] * o_ref[...]


def _ffn(x2, mod5, l, row_of_tile, g_norm, wg, wu, wd, tm, tf):
    M, D = x2.shape
    F = wg.shape[2]
    mod_spec = lambda k: pl.BlockSpec((None, None, None, 1, D), lambda i, f: (l, row_of_tile(i), k, 0, 0))
    return pl.pallas_call(
        functools.partial(_ffn_kernel, tc=512),
        grid=(M // tm, F // tf),
        in_specs=[
            pl.BlockSpec((tm, D), lambda i, f: (i, 0)),
            pl.BlockSpec((1, D), lambda i, f: (0, 0)),
            mod_spec(3), mod_spec(4), mod_spec(5),
            pl.BlockSpec((None, D, tf), lambda i, f: (l, 0, f)),
            pl.BlockSpec((None, D, tf), lambda i, f: (l, 0, f)),
            pl.BlockSpec((None, tf, D), lambda i, f: (l, f, 0)),
        ],
        out_specs=pl.BlockSpec((tm, D), lambda i, f: (i, 0)),
        out_shape=jax.ShapeDtypeStruct((M, D), F32),
        scratch_shapes=[pltpu.VMEM((tm, D), BF16)],
        compiler_params=_cparams(("parallel", "arbitrary")),
        name="swiglu_ffn",
    )(x2, g_norm, mod5, mod5, mod5, wg, wu, wd)


def _rope_tables(S):
    t = jnp.arange(S)
    pos = jnp.stack([t // GRID_W, t % GRID_W], axis=-1).astype(F32)
    n_freq = HEAD_DIM // 4
    inv = ROPE_BASE ** (-jnp.arange(n_freq, dtype=F32) / n_freq)
    ang = pos[:, :, None] * inv
    cos, sin = jnp.cos(ang), jnp.sin(ang)
    cos_h = jnp.concatenate([cos, cos], axis=-1).reshape(S, HEAD_DIM)
    sin_h = jnp.concatenate([-sin, sin], axis=-1).reshape(S, HEAD_DIM)
    reps = MIX_W // HEAD_DIM
    return jnp.tile(cos_h, (1, reps)), jnp.tile(sin_h, (1, reps))


def _split_w_in(w_in):
    sizes = (MIX_W,) * 7 + (SWA_KV_HEADS * HEAD_DIM,) * 2 + (MIX_W,)
    offs = [0]
    for s in sizes:
        offs.append(offs[-1] + s)
    parts = [w_in[..., offs[i]:offs[i + 1]] for i in range(len(sizes))]
    return parts, w_in[..., offs[-1]:]


def _dup_heads(w):
    lead = w.shape[:-1]
    w4 = jnp.repeat(w.reshape(lead + (SWA_KV_HEADS, 1, HEAD_DIM)), 2, axis=-2)
    return w4.reshape(lead + (2 * SWA_KV_HEADS * HEAD_DIM,))


def _tile_heads(g, scale=1.0):
    return jnp.tile(g.astype(F32) * scale, MIX_W // HEAD_DIM)


def _proj_column_vectors(gains):
    rope_blocks = (3, 4, 6, 7)
    half = MIX_W // 2
    rows = []
    for j, g in enumerate(gains):
        gain = jnp.ones((MIX_W,), F32) if g is None else g
        normed = jnp.zeros((MIX_W,), F32) if g is None else jnp.ones((MIX_W,), F32)
        roped = jnp.ones((MIX_W,), F32) if j in rope_blocks else jnp.zeros((MIX_W,), F32)
        if j == 7:
            keep = (jnp.arange(MIX_W) < half).astype(F32)
            gain = jnp.where(keep > 0, gain, 1.0)
            normed, roped = normed * keep, roped * keep
        rows.append(jnp.stack([gain, normed, roped] + [jnp.zeros((MIX_W,), F32)] * 5))
    return jnp.stack(rows)


def _pick_tile(M, unit, target):
    t = min(target, M)
    while M % t or (t > unit and t % unit) or (t < unit and unit % t):
        t //= 2
    return t


def kernel(x, c, ctx, c_ctx, w_ada, b_ada, norm_mix, norm_ffn, w_in, a_q_norm, a_k_norm, a_rpb, b_q_norm, b_k_norm, b_lam_q1, b_lam_k1, b_lam_q2, b_lam_k2, b_subln, c_q_norm, c_k_norm, c_sink, d_w, d_scale, w_branch, w_out, w_ffn_gate, w_ffn_up, w_ffn_down):
    B, S, D = x.shape
    Lc = ctx.shape[1]
    L = w_ada.shape[0]
    assert D == N_BRANCH * MIX_W and S % GRID_W == 0 and S // GRID_W >= NA_ROWS and S >= 3 * SWA_WINDOW

    tm = _pick_tile(S, S, 512)
    tm_proj = _pick_tile(S, S, 1024)
    tm_ffn = _pick_tile(S, S, 1024)
    lat_row_ffn = lambda i: i // (S // tm_ffn)
    tm_merge = _pick_tile(B * S, 8, 1024)
    tmc = _pick_tile(B * Lc, 8, 512)
    lat_row = lambda i: i // (S // tm)
    lat_row_proj = lambda i: i // (S // tm_proj)
    ctx_row = lambda i: B
    n_mod_rows = -(-(B + 1) // 8) * 8

    cc = jnp.zeros((n_mod_rows, D), F32).at[:B].set(c).at[B].set(c_ctx)
    mod = _modulation(cc, w_ada, b_ada).reshape(L, n_mod_rows, 6, 1, D)

    cos_t, sin_t = _rope_tables(S)
    cos_id = jnp.ones((tmc, MIX_W), F32)
    sin_id = jnp.zeros((tmc, MIX_W), F32)
    lane = jnp.arange(2 * LANES)
    bd = (lane[:, None] // HEAD_DIM == lane[None, :] // HEAD_DIM).astype(BF16)
    swa_mask = _swa_mask_table()

    x2 = x.reshape(B * S, D)
    c2 = ctx.reshape(B * Lc, D)
    tn_merge = 512

    (aq, ak, av, bq, bk, bv, sq, sk, sv, du), w_gl = _split_w_in(w_in)
    w_cat = jnp.concatenate([aq, ak, av, bq, bk, bv, sq, _dup_heads(sk), _dup_heads(sv), du], axis=-1).astype(BF16)
    w_gate_r = (w_gl.reshape(L, D, N_BRANCH, D // tn_merge, tn_merge).transpose(0, 1, 3, 2, 4)
                .reshape(L, D, N_BRANCH * D).astype(BF16))
    w_br = w_branch.astype(BF16)
    w_o = w_out.astype(BF16)
    wg, wu, wd = w_ffn_gate.astype(BF16), w_ffn_up.astype(BF16), w_ffn_down.astype(BF16)

    for l in range(L):
        last = l == L - 1
        lam_init = 0.8 - 0.6 * math.exp(-0.3 * l)

        pvecs = _proj_column_vectors([
            _tile_heads(a_q_norm[l], ATTN_SCALE), _tile_heads(a_k_norm[l]), None,
            _tile_heads(b_q_norm[l], ATTN_SCALE), _tile_heads(b_k_norm[l]), None,
            _tile_heads(c_q_norm[l], ATTN_SCALE),
            _tile_heads(c_k_norm[l]),
        ])
        g_mix_norm = norm_mix[l].reshape(1, D)
        g_ffn_norm = norm_ffn[l].reshape(1, D)
        lamp = jnp.stack([b_lam_q1[l], b_lam_k1[l], b_lam_q2[l], b_lam_k2[l]]).astype(F32)
        subln_g = b_subln[l].reshape(1, LANES).astype(F32)
        sink = c_sink[l].astype(F32)
        w_pool = d_w[l].astype(BF16)
        pool_scale = d_scale[l].reshape(1, MIX_W).astype(F32)

        qkv, du_l, h_l = _project(x2, mod, l, lat_row_proj, g_mix_norm, w_cat, pvecs, cos_t, sin_t,
                                  lambda i: i % (S // tm_proj), bd, tm_proj)
        qkv_c, du_c, h_c = _project(c2, mod, l, ctx_row, g_mix_norm, w_cat, pvecs, cos_id, sin_id,
                                    lambda i: 0, bd, tmc)

        o_a = _na_attention(qkv, qkv_c, _na_bias_table(a_rpb[l], S // GRID_W), B, S, Lc)
        o_b = _diff_attention(qkv, qkv_c, lamp, subln_g, lam_init, B, S, Lc)
        o_s = _swa_attention(qkv, qkv_c, sink, swa_mask, B, S, Lc)
        o_d = _pool_mixer(du_l, w_pool, pool_scale, B, S)
        acc = _merge(h_l, (o_a, o_b, o_s, o_d), l, w_gate_r, w_br, tm_merge, tn_merge)
        x2 = _out_project(acc, l, w_o, x2, mod, lat_row, tm)

        if not last:
            oc_a, oc_b, oc_s = _ctx_attention(qkv_c, sink, lamp, subln_g, lam_init, B, Lc)
            oc_d = _pool_mixer(du_c, w_pool, pool_scale, B, Lc)
            acc_c = _merge(h_c, (oc_a, oc_b, oc_s, oc_d), l, w_gate_r, w_br, tmc, tn_merge)
            c2 = _out_project(acc_c, l, w_o, c2, mod, ctx_row, tmc)
            c2 = _ffn(c2, mod, l, ctx_row, g_ffn_norm, wg, wu, wd, tmc, 512)

        x2 = _ffn(x2, mod, l, lat_row_ffn, g_ffn_norm, wg, wu, wd, tm_ffn, 512)

    return x2.reshape(B, S, D)
```

```python
import functools
import math

import numpy as np
import jax
import jax.numpy as jnp
from jax import lax
from jax.experimental import pallas as pl
from jax.experimental.pallas import tpu as pltpu

F32 = jnp.float32
BF16 = jnp.bfloat16

HEAD_DIM = 64
LANES = 128
GRID_W = 64
N_BRANCH = 4
NA_ROWS = 8
NA_COLS = 16
SWA_WINDOW = 128
SWA_KV_HEADS = 2
POOL_WINDOWS = (2, 4, 8, 16)
ROPE_BASE = 10000.0
NORM_EPS = 1e-6
MASK_VALUE = -1e30
ATTN_SCALE = HEAD_DIM ** -0.5
MIX_W = 512
VMEM_LIMIT = 56 * 1024 * 1024


def _cparams(sem):
    return pltpu.CompilerParams(dimension_semantics=sem, vmem_limit_bytes=VMEM_LIMIT)


def _dot(a, b):
    return jnp.dot(a, b, preferred_element_type=F32)


def _dot_nt(a, b):
    return lax.dot_general(a, b, (((1,), (1,)), ((), ())), preferred_element_type=F32)


def _lane_lo(shape):
    return lax.broadcasted_iota(jnp.int32, shape, len(shape) - 1) % LANES < HEAD_DIM


def _stack_halves(q2):
    lo = _lane_lo(q2.shape)
    zero = jnp.zeros_like(q2)
    return jnp.concatenate([jnp.where(lo, q2, zero), jnp.where(lo, zero, q2)], axis=0)


def _merge_halves(o, t):
    lo = _lane_lo((t, LANES))
    return jnp.where(lo, o[:t], o[t:])


def _mod_kernel(c_ref, w_ref, b_ref, o_ref):
    c = c_ref[...]
    act = (c * jax.nn.sigmoid(c)).astype(BF16)
    o_ref[...] = _dot(act, w_ref[...].astype(BF16)) + b_ref[...]


def _modulation(cc, w_ada, b_ada):
    L, D, N = w_ada.shape
    R = cc.shape[0]
    tn = 1024
    return pl.pallas_call(
        _mod_kernel,
        grid=(L, N // tn),
        in_specs=[
            pl.BlockSpec((R, D), lambda l, j: (0, 0)),
            pl.BlockSpec((None, D, tn), lambda l, j: (l, 0, j)),
            pl.BlockSpec((None, 1, tn), lambda l, j: (l, 0, j)),
        ],
        out_specs=pl.BlockSpec((None, R, tn), lambda l, j: (l, 0, j)),
        out_shape=jax.ShapeDtypeStruct((L, R, N), F32),
        compiler_params=_cparams(("parallel", "parallel")),
        name="adaln_mod",
    )(cc, w_ada, b_ada.reshape(L, 1, N))


def _modnorm_into(dst_ref, x_ref, g_ref, sh_ref, sc_ref, chunk=128):
    tm = x_ref.shape[0]
    sh = sh_ref[...]
    gain = g_ref[...] * (1.0 + sc_ref[...])

    def body(r, carry):
        rows = pl.ds(pl.multiple_of(r * chunk, chunk), chunk)
        xf = x_ref[rows, :]
        ms = jnp.mean(xf * xf, axis=-1, keepdims=True)
        dst_ref[rows, :] = (xf * lax.rsqrt(ms + NORM_EPS) * gain + sh).astype(dst_ref.dtype)
        return carry

    lax.fori_loop(0, tm // chunk, body, 0)


def _proj_kernel(x_ref, g_ref, sh_ref, sc_ref, w_ref, pv_ref, cos_ref, sin_ref, bd_ref,
                 qkv_ref, du_ref, h_ref, y_ref, *, plain_steps):
    j = pl.program_id(1)

    @pl.when(j == 0)
    def _():
        _modnorm_into(h_ref, x_ref, g_ref, sh_ref, sc_ref)
        y_ref[...] = jnp.zeros_like(y_ref)

    def norm_rope(yp):
        gain = pv_ref[0:1, :]
        normed = pv_ref[1:2, :] > 0.0
        roped = pv_ref[2:3, :] > 0.0
        y2 = (yp * yp).astype(BF16)
        hw = bd_ref.shape[0]
        ss = jnp.concatenate([_dot(y2[:, c * hw:(c + 1) * hw], bd_ref[...]) for c in range(y2.shape[1] // hw)],
                             axis=1)
        rs = lax.rsqrt(ss * (1.0 / HEAD_DIM) + NORM_EPS)
        n = yp * (jnp.where(normed, rs, 1.0) * gain)
        cos = jnp.where(roped, cos_ref[...], 1.0)
        sin = jnp.where(roped, sin_ref[...], 0.0)
        w = n.shape[-1]
        lane = lax.broadcasted_iota(jnp.int32, n.shape, 1)
        partner = jnp.where((lane % 32) < 16, pltpu.roll(n, w - 16, 1), pltpu.roll(n, 16, 1))
        return n * cos + partner * sin

    def step(epilogue):
        qkv_ref[...] = epilogue(y_ref[...]).astype(BF16)
        y = _dot(h_ref[...], w_ref[...])
        y_ref[...] = y

        @pl.when(j == pl.num_programs(1) - 1)
        def _():
            du_ref[...] = y

    prev_plain = functools.reduce(lambda a, b: a | b, [j == jp for jp in plain_steps])

    @pl.when(prev_plain)
    def _():
        step(lambda yp: yp)

    @pl.when(jnp.logical_not(prev_plain))
    def _():
        step(norm_rope)


_PLAIN_QKV_BLOCKS = (2, 5)


def _project(x2, mod5, l, row_of_tile, g_norm, w_cat, pvecs, cos_t, sin_t, table_block, bd, tm):
    M, D = x2.shape
    nblk = w_cat.shape[2] // MIX_W
    nq = nblk - 1
    mod_spec = lambda k: pl.BlockSpec((None, None, None, 1, D), lambda i, j: (l, row_of_tile(i), k, 0, 0))
    plain_steps = (0,) + tuple(b + 1 for b in _PLAIN_QKV_BLOCKS)
    return pl.pallas_call(
        functools.partial(_proj_kernel, plain_steps=plain_steps),
        grid=(M // tm, nblk),
        in_specs=[
            pl.BlockSpec((tm, D), lambda i, j: (i, 0)),
            pl.BlockSpec((1, D), lambda i, j: (0, 0)),
            mod_spec(0), mod_spec(1),
            pl.BlockSpec((None, D, MIX_W), lambda i, j: (l, 0, j)),
            pl.BlockSpec((None, 8, MIX_W), lambda i, j: (jnp.maximum(j - 1, 0), 0, 0)),
            pl.BlockSpec((tm, MIX_W), lambda i, j: (table_block(i), 0)),
            pl.BlockSpec((tm, MIX_W), lambda i, j: (table_block(i), 0)),
            pl.BlockSpec(bd.shape, lambda i, j: (0, 0)),
        ],
        out_specs=[
            pl.BlockSpec((tm, MIX_W), lambda i, j: (i, jnp.maximum(j - 1, 0))),
            pl.BlockSpec((tm, MIX_W), lambda i, j: (i, 0)),
            pl.BlockSpec((tm, D), lambda i, j: (i, 0)),
        ],
        out_shape=[
            jax.ShapeDtypeStruct((M, nq * MIX_W), BF16),
            jax.ShapeDtypeStruct((M, MIX_W), F32),
            jax.ShapeDtypeStruct((M, D), BF16),
        ],
        scratch_shapes=[pltpu.VMEM((tm, MIX_W), F32)],
        compiler_params=_cparams(("parallel", "arbitrary")),
        name="in_proj",
    )(x2, g_norm, mod5, mod5, w_cat, pvecs, cos_t, sin_t, bd)


_QA, _KA, _VA = 0, 4, 8
_QB, _KB, _VB = 12, 16, 20
_QC = 24
_KC, _VC = 28, 30


def _softmax_pv(parts, extra_logit=None):
    s = jnp.concatenate([s for s, _ in parts], axis=1) if len(parts) > 1 else parts[0][0]
    m = s.max(axis=-1, keepdims=True)
    if extra_logit is not None:
        m = jnp.maximum(m, extra_logit)
    p = jnp.exp(s - m)
    l = p.sum(axis=-1, keepdims=True)
    if extra_logit is not None:
        l = l + jnp.exp(extra_logit - m)
    pb = p.astype(BF16)
    o = None
    start = 0
    for sp, v in parts:
        width = sp.shape[1]
        term = _dot(pb[:, start:start + width], v)
        o = term if o is None else o + term
        start += width
    return o / l


NA_GROUP = 4
NA_UNION = NA_GROUP + NA_ROWS


def _na_kernel(q_ref, k_ref, v_ref, kc_ref, vc_ref, bias_ref, o_ref, *, rows, units):
    g0 = pl.program_id(2) * units
    n_groups = rows // NA_GROUP
    tq = NA_GROUP * GRID_W
    kc = kc_ref[...]
    vc = vc_ref[...]
    scores = []
    for u in range(units):
        g = g0 + u
        start = jnp.clip(NA_GROUP * g - NA_ROWS // 2, 0, rows - NA_UNION)
        variant = jnp.where(g == 0, 0, jnp.where(g == n_groups - 1, 2, 1))
        krows = pl.ds(pl.multiple_of(start * GRID_W, GRID_W), NA_UNION * GRID_W)
        qs = _stack_halves(q_ref[u * tq:(u + 1) * tq, :])
        s = _dot_nt(qs, k_ref[krows, :]) + bias_ref[variant]
        sc = _dot_nt(qs, kc)
        scores.append((s, sc, krows))
    for u, (s, sc, krows) in enumerate(scores):
        o = _softmax_pv([(s, v_ref[krows, :]), (sc, vc)])
        o_ref[u * tq:(u + 1) * tq, :] = _merge_halves(o, tq).astype(o_ref.dtype)


def _na_attention(qkv, qkv_c, bias, B, S, Lc):
    rows = S // GRID_W
    assert rows % NA_GROUP == 0 and rows >= NA_UNION
    n_groups = rows // NA_GROUP
    units = 8 if n_groups % 8 == 0 else (4 if n_groups % 4 == 0 else 1)
    nstep = n_groups // units
    tq = units * NA_GROUP * GRID_W
    npair = MIX_W // LANES
    return pl.pallas_call(
        functools.partial(_na_kernel, rows=rows, units=units),
        grid=(B, npair, nstep),
        in_specs=[
            pl.BlockSpec((tq, LANES), lambda b, p, r: (b * nstep + r, _QA + p)),
            pl.BlockSpec((S, LANES), lambda b, p, r: (b, _KA + p)),
            pl.BlockSpec((S, LANES), lambda b, p, r: (b, _VA + p)),
            pl.BlockSpec((Lc, LANES), lambda b, p, r: (b, _KA + p)),
            pl.BlockSpec((Lc, LANES), lambda b, p, r: (b, _VA + p)),
            pl.BlockSpec((None,) + bias.shape[1:], lambda b, p, r: (p, 0, 0, 0)),
        ],
        out_specs=pl.BlockSpec((tq, LANES), lambda b, p, r: (b * nstep + r, p)),
        out_shape=jax.ShapeDtypeStruct((B * S, MIX_W), BF16),
        compiler_params=_cparams(("parallel", "parallel", "arbitrary")),
        name="na_attention",
    )(qkv, qkv, qkv, qkv_c, qkv_c, bias)


def _na_bias_kernel(rpb_ref, onehot_ref, valid_ref, o_ref):
    r = rpb_ref[...]
    hi = r.astype(BF16)
    r1 = r - hi.astype(F32)
    mid = r1.astype(BF16)
    lo = (r1 - mid.astype(F32)).astype(BF16)
    oh = onehot_ref[...]
    t = _dot(hi, oh) + _dot(mid, oh) + _dot(lo, oh)
    o_ref[...] = jnp.where(valid_ref[...] > 0.0, t, MASK_VALUE)


def _na_window_pattern(rows, g):
    start_u = np.clip(NA_GROUP * g - NA_ROWS // 2, 0, rows - NA_UNION)
    r = NA_GROUP * g + np.arange(NA_GROUP)[:, None]
    start_r = np.clip(r - NA_ROWS // 2, 0, rows - NA_ROWS)
    kr = start_u + np.arange(NA_UNION)[None, :]
    valid = (kr >= start_r) & (kr < start_r + NA_ROWS)
    return np.where(valid, kr - r + (NA_ROWS - 1), 0), valid


def _na_bias_table(rpb, rows):
    H, n_dr, n_dc = rpb.shape
    n_groups = rows // NA_GROUP
    patterns = [_na_window_pattern(rows, g) for g in (0, 1, n_groups - 1)]
    for g in range(1, n_groups - 1):
        dr_g, valid_g = _na_window_pattern(rows, g)
        assert (dr_g == patterns[1][0]).all() and (valid_g == patterns[1][1]).all()
    col = np.arange(GRID_W)
    col_start = np.clip(col - NA_COLS // 2, 0, GRID_W - NA_COLS)
    valid = (col[None, :] >= col_start[:, None]) & (col[None, :] < col_start[:, None] + NA_COLS)
    dc = np.clip(col[None, :] - col[:, None], -(NA_COLS - 1), NA_COLS - 1) + (NA_COLS - 1)
    onehot = (np.arange(LANES)[:, None] == dc.reshape(1, -1)).astype(np.float32)
    n_rows = -(-(H * n_dr) // LANES) * LANES
    rpb_rows = jnp.zeros((n_rows, LANES), F32).at[:H * n_dr, :n_dc].set(rpb.reshape(H * n_dr, n_dc).astype(F32))
    t = pl.pallas_call(
        _na_bias_kernel,
        out_shape=jax.ShapeDtypeStruct((n_rows, GRID_W * GRID_W), F32),
        name="na_bias",
    )(rpb_rows, jnp.asarray(onehot, BF16), jnp.asarray(valid.reshape(1, -1), F32))
    t3 = t.reshape(n_rows, GRID_W, GRID_W)
    out_rows, out_cols = 2 * NA_GROUP * GRID_W, NA_UNION * GRID_W
    return pl.pallas_call(
        functools.partial(_na_assemble_kernel, patterns=patterns, n_dr=n_dr),
        grid=(H // 2,),
        in_specs=[pl.BlockSpec(t3.shape, lambda p: (0, 0, 0))],
        out_specs=pl.BlockSpec((None, len(patterns), out_rows, out_cols), lambda p: (p, 0, 0, 0)),
        out_shape=jax.ShapeDtypeStruct((H // 2, len(patterns), out_rows, out_cols), F32),
        compiler_params=_cparams(("parallel",)),
        name="na_bias_assemble",
    )(t3)


def _na_assemble_kernel(t3_ref, o_ref, *, patterns, n_dr):
    p = pl.program_id(0)
    masked = jnp.full((GRID_W, GRID_W), MASK_VALUE, F32)
    for v, (dr, ok) in enumerate(patterns):
        for parity in range(2):
            base = (2 * p + parity) * n_dr
            for rr in range(NA_GROUP):
                r0 = (parity * NA_GROUP + rr) * GRID_W
                for j in range(NA_UNION):
                    tile = t3_ref[base + int(dr[rr, j])] if ok[rr, j] else masked
                    o_ref[v, r0:r0 + GRID_W, j * GRID_W:(j + 1) * GRID_W] = tile


def _lambda_full(lamp_ref, lam_init):
    lp = lamp_ref[...]
    a = jnp.sum(lp[0:1] * lp[1:2], axis=-1, keepdims=True)
    b = jnp.sum(lp[2:3] * lp[3:4], axis=-1, keepdims=True)
    return jnp.exp(a) - jnp.exp(b) + lam_init


def _subln(o, g, lam_init):
    ms = jnp.mean(o * o, axis=-1, keepdims=True)
    return o * lax.rsqrt(ms + NORM_EPS) * g * (1.0 - lam_init)


def _fold_lanes(x, op):
    out = x[:, :LANES]
    for b in range(1, x.shape[1] // LANES):
        out = op(out, x[:, b * LANES:(b + 1) * LANES])
    return out


def _diff_kernel(q_ref, k_ref, v_ref, kc_ref, vc_ref, lamp_ref, g_ref, o_ref, s_ref, *, lam_init, tk):
    tq = q_ref.shape[0]
    S = k_ref.shape[0]
    nkb = S // tk
    lam = _lambda_full(lamp_ref, lam_init)
    nh = q_ref.shape[1] // LANES
    heads = [slice(h * LANES, (h + 1) * LANES) for h in range(nh)]
    qs = [_stack_halves(q_ref[:, hs]) for hs in heads]
    lane_max, m, lane_sum, acc = [None] * nh, [None] * nh, [None] * nh, [None] * nh

    def pass1_block(h, kb):
        s = _dot_nt(qs[h], k_ref[kb * tk:(kb + 1) * tk, heads[h]])
        s_ref[h, kb] = s
        blk = _fold_lanes(s, jnp.maximum)
        lane_max[h] = blk if lane_max[h] is None else jnp.maximum(lane_max[h], blk)

    def pass1_finish(h):
        sc = _dot_nt(qs[h], kc_ref[:, heads[h]])
        m[h] = jnp.maximum(lane_max[h], _fold_lanes(sc, jnp.maximum)).max(axis=-1, keepdims=True)
        p = jnp.exp(sc - m[h])
        lane_sum[h] = _fold_lanes(p, jnp.add)
        acc[h] = _dot(p.astype(BF16), vc_ref[:, heads[h]])

    def pass2_block(h, kb):
        p = jnp.exp(s_ref[h, kb] - m[h])
        lane_sum[h] = lane_sum[h] + _fold_lanes(p, jnp.add)
        acc[h] = acc[h] + _dot(p.astype(BF16), v_ref[kb * tk:(kb + 1) * tk, heads[h]])

    def pass2_finish(h):
        on = acc[h] / lane_sum[h].sum(axis=-1, keepdims=True)
        o = on[:tq] - lam * on[tq:]
        o_ref[:, heads[h]] = _subln(o, g_ref[...], lam_init).astype(o_ref.dtype)

    for t in range(nh + 1):
        for kb in range(nkb):
            if t < nh:
                pass1_block(t, kb)
            if t >= 1:
                pass2_block(t - 1, kb)
        if t < nh:
            pass1_finish(t)
        if t >= 1:
            pass2_finish(t - 1)


def _diff_attention(qkv, qkv_c, lamp, subln_g, lam_init, B, S, Lc):
    tq = 256
    nq = S // tq
    hw = 4 * LANES
    ngrp = MIX_W // hw
    qb, kb, vb = _QB * LANES // hw, _KB * LANES // hw, _VB * LANES // hw
    tk = 512
    nh = hw // LANES
    return pl.pallas_call(
        functools.partial(_diff_kernel, lam_init=lam_init, tk=tk),
        grid=(B, ngrp, nq),
        in_specs=[
            pl.BlockSpec((tq, hw), lambda b, h, i: (b * nq + i, qb + h)),
            pl.BlockSpec((S, hw), lambda b, h, i: (b, kb + h)),
            pl.BlockSpec((S, hw), lambda b, h, i: (b, vb + h)),
            pl.BlockSpec((Lc, hw), lambda b, h, i: (b, kb + h)),
            pl.BlockSpec((Lc, hw), lambda b, h, i: (b, vb + h)),
            pl.BlockSpec((4, HEAD_DIM), lambda b, h, i: (0, 0)),
            pl.BlockSpec((1, LANES), lambda b, h, i: (0, 0)),
        ],
        out_specs=pl.BlockSpec((tq, hw), lambda b, h, i: (b * nq + i, h)),
        out_shape=jax.ShapeDtypeStruct((B * S, MIX_W), BF16),
        scratch_shapes=[pltpu.VMEM((nh, S // tk, 2 * tq, tk), F32)],
        compiler_params=_cparams(("parallel", "parallel", "arbitrary")),
        name="diff_attention",
    )(qkv, qkv, qkv, qkv_c, qkv_c, lamp, subln_g)


def _stack_group(q4):
    return jnp.concatenate([_stack_halves(q4[:, :LANES]), _stack_halves(q4[:, LANES:])], axis=0)


def _merge_group(o, t):
    return jnp.concatenate([_merge_halves(o[:2 * t], t), _merge_halves(o[2 * t:], t)], axis=-1)


def _sink_column(sink_ref, c, t):
    group = sink_ref.shape[0] // SWA_KV_HEADS
    return jnp.concatenate([jnp.full((t, 1), sink_ref[c * group + g], F32) for g in range(group)], axis=0)


def _swa_kernel(sink_ref, q_ref, k_ref, v_ref, kc_ref, vc_ref, mask_ref, o_ref, *, qb):
    c = pl.program_id(1)
    n0 = pl.program_id(2) * qb
    W = SWA_WINDOW
    S = k_ref.shape[0]
    nb = S // W
    kc = kc_ref[...]
    vc = vc_ref[...]
    sink_col = _sink_column(sink_ref, c, W)
    group = sink_ref.shape[0] // SWA_KV_HEADS
    scores = []
    for t in range(qb):
        n = n0 + t
        ks = pl.multiple_of(jnp.clip((n - 1) * W, 0, S - 3 * W), W)
        krows = pl.ds(ks, 3 * W)
        variant = jnp.where(n == 0, 0, jnp.where(n == nb - 1, 2, 1))
        mask = mask_ref[variant]
        qs = _stack_group(q_ref[t * W:(t + 1) * W, :])
        s = _dot_nt(qs, k_ref[krows, :]) + jnp.concatenate([mask] * group, axis=0)
        sc = _dot_nt(qs, kc)
        scores.append((s, sc, krows))
    for t, (s, sc, krows) in enumerate(scores):
        o = _softmax_pv([(s, v_ref[krows, :]), (sc, vc)], extra_logit=sink_col)
        o_ref[t * W:(t + 1) * W, :] = _merge_group(o, W).astype(o_ref.dtype)


def _swa_mask_table():
    W = SWA_WINDOW
    i = np.arange(W)[:, None]
    j = np.arange(3 * W)[None, :]
    tabs = [np.where(np.abs(j - shift - i) <= W, 0.0, MASK_VALUE) for shift in (0, W, 2 * W)]
    return jnp.asarray(np.stack(tabs), F32)


def _swa_attention(qkv, qkv_c, sink, mask, B, S, Lc):
    W = SWA_WINDOW
    nb = S // W
    qb = 16 if nb % 16 == 0 else (8 if nb % 8 == 0 else 1)
    nqb = nb // qb
    return pl.pallas_call(
        functools.partial(_swa_kernel, qb=qb),
        grid=(B, SWA_KV_HEADS, nqb),
        in_specs=[
            pl.BlockSpec(memory_space=pltpu.SMEM),
            pl.BlockSpec((qb * W, 2 * LANES), lambda b, c, n: (b * nqb + n, _QC // 2 + c)),
            pl.BlockSpec((S, LANES), lambda b, c, n: (b, _KC + c)),
            pl.BlockSpec((S, LANES), lambda b, c, n: (b, _VC + c)),
            pl.BlockSpec((Lc, LANES), lambda b, c, n: (b, _KC + c)),
            pl.BlockSpec((Lc, LANES), lambda b, c, n: (b, _VC + c)),
            pl.BlockSpec(mask.shape, lambda b, c, n: (0, 0, 0)),
        ],
        out_specs=pl.BlockSpec((qb * W, 2 * LANES), lambda b, c, n: (b * nqb + n, c)),
        out_shape=jax.ShapeDtypeStruct((B * S, MIX_W), BF16),
        compiler_params=_cparams(("parallel", "parallel", "arbitrary")),
        name="swa_attention",
    )(sink, qkv, qkv, qkv, qkv_c, qkv_c, mask)


def _ctx_attn_kernel(sink_ref, qkv_ref, lamp_ref, g_ref, oa_ref, ob_ref, os_ref, *, lam_init):
    Lc = qkv_ref.shape[0]

    def cols(blk, width=LANES):
        return qkv_ref[:, blk * LANES: blk * LANES + width]

    for p in range(MIX_W // LANES):
        qs = _stack_halves(cols(_QA + p))
        o = _softmax_pv([(_dot_nt(qs, cols(_KA + p)), cols(_VA + p))])
        oa_ref[:, p * LANES:(p + 1) * LANES] = _merge_halves(o, Lc).astype(oa_ref.dtype)

    lam = _lambda_full(lamp_ref, lam_init)
    for h in range(MIX_W // LANES):
        qs = _stack_halves(cols(_QB + h))
        on = _softmax_pv([(_dot_nt(qs, cols(_KB + h)), cols(_VB + h))])
        o = on[:Lc] - lam * on[Lc:]
        ob_ref[:, h * LANES:(h + 1) * LANES] = _subln(o, g_ref[...], lam_init).astype(ob_ref.dtype)

    for c in range(SWA_KV_HEADS):
        qs = _stack_group(cols(_QC + 2 * c, 2 * LANES))
        o = _softmax_pv([(_dot_nt(qs, cols(_KC + c)), cols(_VC + c))],
                        extra_logit=_sink_column(sink_ref, c, Lc))
        os_ref[:, c * 2 * LANES:(c + 1) * 2 * LANES] = _merge_group(o, Lc).astype(os_ref.dtype)


def _ctx_attention(qkv_c, sink, lamp, subln_g, lam_init, B, Lc):
    NQ = qkv_c.shape[1]
    out = jax.ShapeDtypeStruct((B * Lc, MIX_W), BF16)
    ospec = pl.BlockSpec((Lc, MIX_W), lambda b: (b, 0))
    return pl.pallas_call(
        functools.partial(_ctx_attn_kernel, lam_init=lam_init),
        grid=(B,),
        in_specs=[
            pl.BlockSpec(memory_space=pltpu.SMEM),
            pl.BlockSpec((Lc, NQ), lambda b: (b, 0)),
            pl.BlockSpec((4, HEAD_DIM), lambda b: (0, 0)),
            pl.BlockSpec((1, LANES), lambda b: (0, 0)),
        ],
        out_specs=[ospec, ospec, ospec],
        out_shape=[out, out, out],
        compiler_params=_cparams(("parallel",)),
        name="ctx_attention",
    )(sink, qkv_c, lamp, subln_g)


def _shift_down(x, k, t_idx):
    return jnp.where(t_idx >= k, pltpu.roll(x, k, 0), 0.0)


def _shift_up(x, k, t_idx):
    T = x.shape[0]
    return jnp.where(t_idx < T - k, pltpu.roll(x, T - k, 0), 0.0)


def _pool_kernel(u_ref, w_ref, scale_ref, o_ref):
    T = u_ref.shape[0]
    pg = u_ref.shape[1] // len(POOL_WINDOWS)
    t_idx = lax.broadcasted_iota(jnp.int32, (T, pg), 0)
    for g, w in enumerate(POOL_WINDOWS):
        u = u_ref[:, g * pg:(g + 1) * pg]
        back = u
        fwd = u
        span = 1
        while span < w // 2:
            back = back + _shift_down(back, span, t_idx)
            fwd = fwd + _shift_up(fwd, span, t_idx)
            span *= 2
        win = _shift_down(back, 1, t_idx) + fwd
        lo = jnp.clip(t_idx - w // 2, 0, T)
        hi = jnp.clip(t_idx - w // 2 + w, 0, T)
        d = (win / (hi - lo).astype(F32) - u).astype(BF16)
        mixed = _dot(d, w_ref[g])
        o_ref[:, g * pg:(g + 1) * pg] = (mixed * scale_ref[:, g * pg:(g + 1) * pg]).astype(o_ref.dtype)


def _pool_mixer(du, w_grp, scale, nseq, T):
    return pl.pallas_call(
        _pool_kernel,
        grid=(nseq,),
        in_specs=[
            pl.BlockSpec((T, MIX_W), lambda b: (b, 0)),
            pl.BlockSpec(w_grp.shape, lambda b: (0, 0, 0)),
            pl.BlockSpec((1, MIX_W), lambda b: (0, 0)),
        ],
        out_specs=pl.BlockSpec((T, MIX_W), lambda b: (b, 0)),
        out_shape=jax.ShapeDtypeStruct((nseq * T, MIX_W), BF16),
        compiler_params=_cparams(("parallel",)),
        name="pool_mixer",
    )(du, w_grp, scale)


def _merge_kernel(h_ref, oa_ref, ob_ref, os_ref, od_ref, wg0_ref, wg1_ref, wg2_ref, wg3_ref, wb_ref,
                  acc_ref, f32_ref):
    h = h_ref[...]
    branches = zip((oa_ref, ob_ref, os_ref, od_ref), (wg0_ref, wg1_ref, wg2_ref, wg3_ref))
    for n, (o_ref, wg_ref) in enumerate(branches):
        gate = jax.nn.sigmoid(_dot(h, wg_ref[...]))
        term = gate * _dot(o_ref[...], wb_ref[n])
        if n == 0:
            f32_ref[...] = term
        else:
            f32_ref[...] += term
    acc_ref[...] = f32_ref[...].astype(acc_ref.dtype)


def _merge(h, outs, l, w_gate, w_br, tm, tn):
    M, D = h.shape
    nj = D // tn
    ospec = pl.BlockSpec((tm, MIX_W), lambda i, j: (i, 0))
    gate_spec = lambda n: pl.BlockSpec((None, D, tn), lambda i, j: (l, 0, n * nj + j))
    return pl.pallas_call(
        _merge_kernel,
        grid=(M // tm, nj),
        in_specs=[
            pl.BlockSpec((tm, D), lambda i, j: (i, 0)),
            ospec, ospec, ospec, ospec,
            gate_spec(0), gate_spec(1), gate_spec(2), gate_spec(3),
            pl.BlockSpec((None, N_BRANCH, MIX_W, tn), lambda i, j: (l, 0, 0, j)),
        ],
        out_specs=pl.BlockSpec((tm, tn), lambda i, j: (i, j)),
        out_shape=jax.ShapeDtypeStruct((M, D), BF16),
        scratch_shapes=[pltpu.VMEM((tm, tn), F32)],
        compiler_params=_cparams(("parallel", "arbitrary")),
        name="gated_merge",
    )(h, *outs, w_gate, w_gate, w_gate, w_gate, w_br)


def _outproj_kernel(a_ref, w_ref, x_ref, g_ref, o_ref, *, tc):
    a = a_ref[...]
    for c in range(o_ref.shape[1] // tc):
        cs = slice(c * tc, (c + 1) * tc)
        o_ref[:, cs] = x_ref[:, cs] + g_ref[:, cs] * _dot(a, w_ref[:, cs])


def _out_project(acc, l, w_o, x2, mod5, row_of_tile, tm):
    M, D = x2.shape
    return pl.pallas_call(
        functools.partial(_outproj_kernel, tc=512),
        grid=(M // tm,),
        in_specs=[
            pl.BlockSpec((tm, D), lambda i: (i, 0)),
            pl.BlockSpec((None, D, D), lambda i: (l, 0, 0), pipeline_mode=pl.Buffered(1)),
            pl.BlockSpec((tm, D), lambda i: (i, 0)),
            pl.BlockSpec((None, None, None, 1, D), lambda i: (l, row_of_tile(i), 2, 0, 0)),
        ],
        out_specs=pl.BlockSpec((tm, D), lambda i: (i, 0)),
        out_shape=jax.ShapeDtypeStruct((M, D), F32),
        compiler_params=_cparams(("parallel",)),
        name="out_proj",
    )(acc, w_o, x2, mod5)


def _ffn_kernel(x_ref, g_ref, sh_ref, sc_ref, gate_ref, wg_ref, wu_ref, wd_ref, o_ref, h_ref, *, tc, nf):
    f = pl.program_id(1)
    D = o_ref.shape[1]

    last = nf - 1

    @pl.when(f == 0)
    def _():
        _modnorm_into(h_ref, x_ref, g_ref, sh_ref, sc_ref)

    def step(first, final):
        h = h_ref[...]
        a = _dot(h, wg_ref[...])
        u = _dot(h, wu_ref[...])
        act = (a * jax.nn.sigmoid(a) * u).astype(BF16)
        for c in range(D // tc):
            cs = slice(c * tc, (c + 1) * tc)
            part = _dot(act, wd_ref[:, cs])
            total = part if first else o_ref[:, cs] + part
            o_ref[:, cs] = x_ref[:, cs] + gate_ref[:, cs] * total if final else total

    if nf == 1:
        step(True, True)
        return

    @pl.when(f == 0)
    def _():
        step(True, False)

    @pl.when((f > 0) & (f < last))
    def _():
        step(False, False)

    @pl.when(f == last)
    def _():
        step(False, True)


def _ffn(x2, mod5, l, row_of_tile, g_norm, wg, wu, wd, tm, tf):
    M, D = x2.shape
    F = wg.shape[2]
    mod_spec = lambda k: pl.BlockSpec((None, None, None, 1, D), lambda i, f: (l, row_of_tile(i), k, 0, 0))
    return pl.pallas_call(
        functools.partial(_ffn_kernel, tc=512, nf=F // tf),
        grid=(M // tm, F // tf),
        in_specs=[
            pl.BlockSpec((tm, D), lambda i, f: (i, 0)),
            pl.BlockSpec((1, D), lambda i, f: (0, 0)),
            mod_spec(3), mod_spec(4), mod_spec(5),
            pl.BlockSpec((None, D, tf), lambda i, f: (l, 0, f)),
            pl.BlockSpec((None, D, tf), lambda i, f: (l, 0, f)),
            pl.BlockSpec((None, tf, D), lambda i, f: (l, f, 0)),
        ],
        out_specs=pl.BlockSpec((tm, D), lambda i, f: (i, 0)),
        out_shape=jax.ShapeDtypeStruct((M, D), F32),
        scratch_shapes=[pltpu.VMEM((tm, D), BF16)],
        compiler_params=_cparams(("parallel", "arbitrary")),
        name="swiglu_ffn",
    )(x2, g_norm, mod5, mod5, mod5, wg, wu, wd)


def _rope_tables(S):
    t = jnp.arange(S)
    pos = jnp.stack([t // GRID_W, t % GRID_W], axis=-1).astype(F32)
    n_freq = HEAD_DIM // 4
    inv = ROPE_BASE ** (-jnp.arange(n_freq, dtype=F32) / n_freq)
    ang = pos[:, :, None] * inv
    cos, sin = jnp.cos(ang), jnp.sin(ang)
    cos_h = jnp.concatenate([cos, cos], axis=-1).reshape(S, HEAD_DIM)
    sin_h = jnp.concatenate([-sin, sin], axis=-1).reshape(S, HEAD_DIM)
    reps = MIX_W // HEAD_DIM
    return jnp.tile(cos_h, (1, reps)), jnp.tile(sin_h, (1, reps))


def _split_w_in(w_in):
    sizes = (MIX_W,) * 7 + (SWA_KV_HEADS * HEAD_DIM,) * 2 + (MIX_W,)
    offs = [0]
    for s in sizes:
        offs.append(offs[-1] + s)
    parts = [w_in[..., offs[i]:offs[i + 1]] for i in range(len(sizes))]
    return parts, w_in[..., offs[-1]:]


def _dup_heads(w):
    lead = w.shape[:-1]
    w4 = jnp.repeat(w.reshape(lead + (SWA_KV_HEADS, 1, HEAD_DIM)), 2, axis=-2)
    return w4.reshape(lead + (2 * SWA_KV_HEADS * HEAD_DIM,))


def _tile_heads(g, scale=1.0):
    return jnp.tile(g.astype(F32) * scale, MIX_W // HEAD_DIM)


def _proj_column_vectors(gains):
    rope_blocks = (3, 4, 6, 7)
    half = MIX_W // 2
    rows = []
    for j, g in enumerate(gains):
        gain = jnp.ones((MIX_W,), F32) if g is None else g
        normed = jnp.zeros((MIX_W,), F32) if g is None else jnp.ones((MIX_W,), F32)
        roped = jnp.ones((MIX_W,), F32) if j in rope_blocks else jnp.zeros((MIX_W,), F32)
        if j == 7:
            keep = (jnp.arange(MIX_W) < half).astype(F32)
            gain = jnp.where(keep > 0, gain, 1.0)
            normed, roped = normed * keep, roped * keep
        rows.append(jnp.stack([gain, normed, roped] + [jnp.zeros((MIX_W,), F32)] * 5))
    return jnp.stack(rows)


def _pick_tile(M, unit, target):
    t = min(target, M)
    while M % t or (t > unit and t % unit) or (t < unit and unit % t):
        t //= 2
    return t


def kernel(x, c, ctx, c_ctx, w_ada, b_ada, norm_mix, norm_ffn, w_in, a_q_norm, a_k_norm, a_rpb, b_q_norm, b_k_norm, b_lam_q1, b_lam_k1, b_lam_q2, b_lam_k2, b_subln, c_q_norm, c_k_norm, c_sink, d_w, d_scale, w_branch, w_out, w_ffn_gate, w_ffn_up, w_ffn_down):
    B, S, D = x.shape
    Lc = ctx.shape[1]
    L = w_ada.shape[0]
    assert D == N_BRANCH * MIX_W and S % GRID_W == 0 and S // GRID_W >= NA_ROWS and S >= 3 * SWA_WINDOW

    tm = _pick_tile(S, S, 512)
    tm_proj = _pick_tile(S, S, 1024)
    tm_ffn = _pick_tile(S, S, 1024)
    lat_row_ffn = lambda i: i // (S // tm_ffn)
    tm_merge = _pick_tile(B * S, 8, 1024)
    tmc = _pick_tile(B * Lc, 8, 512)
    lat_row = lambda i: i // (S // tm)
    lat_row_proj = lambda i: i // (S // tm_proj)
    ctx_row = lambda i: B
    n_mod_rows = -(-(B + 1) // 8) * 8

    cc = jnp.zeros((n_mod_rows, D), F32).at[:B].set(c).at[B].set(c_ctx)
    mod = _modulation(cc, w_ada, b_ada).reshape(L, n_mod_rows, 6, 1, D)

    cos_t, sin_t = _rope_tables(S)
    cos_id = jnp.ones((tmc, MIX_W), F32)
    sin_id = jnp.zeros((tmc, MIX_W), F32)
    lane = jnp.arange(2 * LANES)
    bd = (lane[:, None] // HEAD_DIM == lane[None, :] // HEAD_DIM).astype(BF16)
    swa_mask = _swa_mask_table()

    x2 = x.reshape(B * S, D)
    c2 = ctx.reshape(B * Lc, D)
    tn_merge = 512

    (aq, ak, av, bq, bk, bv, sq, sk, sv, du), w_gl = _split_w_in(w_in)
    w_cat = jnp.concatenate([aq, ak, av, bq, bk, bv, sq, _dup_heads(sk), _dup_heads(sv), du], axis=-1).astype(BF16)
    w_gate_r = w_gl.astype(BF16)
    w_br = w_branch.astype(BF16)
    w_o = w_out.astype(BF16)
    wg, wu, wd = w_ffn_gate.astype(BF16), w_ffn_up.astype(BF16), w_ffn_down.astype(BF16)

    for l in range(L):
        last = l == L - 1
        lam_init = 0.8 - 0.6 * math.exp(-0.3 * l)

        pvecs = _proj_column_vectors([
            _tile_heads(a_q_norm[l], ATTN_SCALE), _tile_heads(a_k_norm[l]), None,
            _tile_heads(b_q_norm[l], ATTN_SCALE), _tile_heads(b_k_norm[l]), None,
            _tile_heads(c_q_norm[l], ATTN_SCALE),
            _tile_heads(c_k_norm[l]),
        ])
        g_mix_norm = norm_mix[l].reshape(1, D)
        g_ffn_norm = norm_ffn[l].reshape(1, D)
        lamp = jnp.stack([b_lam_q1[l], b_lam_k1[l], b_lam_q2[l], b_lam_k2[l]]).astype(F32)
        subln_g = b_subln[l].reshape(1, LANES).astype(F32)
        sink = c_sink[l].astype(F32)
        w_pool = d_w[l].astype(BF16)
        pool_scale = d_scale[l].reshape(1, MIX_W).astype(F32)

        qkv, du_l, h_l = _project(x2, mod, l, lat_row_proj, g_mix_norm, w_cat, pvecs, cos_t, sin_t,
                                  lambda i: i % (S // tm_proj), bd, tm_proj)
        qkv_c, du_c, h_c = _project(c2, mod, l, ctx_row, g_mix_norm, w_cat, pvecs, cos_id, sin_id,
                                    lambda i: 0, bd, tmc)

        o_a = _na_attention(qkv, qkv_c, _na_bias_table(a_rpb[l], S // GRID_W), B, S, Lc)
        o_b = _diff_attention(qkv, qkv_c, lamp, subln_g, lam_init, B, S, Lc)
        o_s = _swa_attention(qkv, qkv_c, sink, swa_mask, B, S, Lc)
        o_d = _pool_mixer(du_l, w_pool, pool_scale, B, S)
        acc = _merge(h_l, (o_a, o_b, o_s, o_d), l, w_gate_r, w_br, tm_merge, tn_merge)
        x2 = _out_project(acc, l, w_o, x2, mod, lat_row, tm)

        if not last:
            oc_a, oc_b, oc_s = _ctx_attention(qkv_c, sink, lamp, subln_g, lam_init, B, Lc)
            oc_d = _pool_mixer(du_c, w_pool, pool_scale, B, Lc)
            acc_c = _merge(h_c, (oc_a, oc_b, oc_s, oc_d), l, w_gate_r, w_br, tmc, tn_merge)
            c2 = _out_project(acc_c, l, w_o, c2, mod, ctx_row, tmc)
            c2 = _ffn(c2, mod, l, ctx_row, g_ffn_norm, wg, wu, wd, tmc, 512)

        x2 = _ffn(x2, mod, l, lat_row_ffn, g_ffn_norm, wg, wu, wd, tm_ffn, 512)

    return x2.reshape(B, S, D)
```

```python
import functools
import math

import numpy as np
import jax
import jax.numpy as jnp
from jax import lax
from jax.experimental import pallas as pl
from jax.experimental.pallas import tpu as pltpu

F32 = jnp.float32
BF16 = jnp.bfloat16

HEAD_DIM = 64
LANES = 128
GRID_W = 64
N_BRANCH = 4
NA_ROWS = 8
NA_COLS = 16
SWA_WINDOW = 128
SWA_KV_HEADS = 2
POOL_WINDOWS = (2, 4, 8, 16)
ROPE_BASE = 10000.0
NORM_EPS = 1e-6
MASK_VALUE = -1e30
ATTN_SCALE = HEAD_DIM ** -0.5
LOG2E = math.log2(math.e)
Q_SCALE = ATTN_SCALE * LOG2E
MIX_W = 512
VMEM_LIMIT = 56 * 1024 * 1024


def _cparams(sem):
    return pltpu.CompilerParams(dimension_semantics=sem, vmem_limit_bytes=VMEM_LIMIT)


def _dot(a, b):
    return jnp.dot(a, b, preferred_element_type=F32)


def _dot_nt(a, b):
    return lax.dot_general(a, b, (((1,), (1,)), ((), ())), preferred_element_type=F32)


def _lane_lo(shape):
    return lax.broadcasted_iota(jnp.int32, shape, len(shape) - 1) % LANES < HEAD_DIM


def _stack_halves(q2):
    lo = _lane_lo(q2.shape)
    zero = jnp.zeros_like(q2)
    return jnp.concatenate([jnp.where(lo, q2, zero), jnp.where(lo, zero, q2)], axis=0)


def _merge_halves(o, t):
    lo = _lane_lo((t, LANES))
    return jnp.where(lo, o[:t], o[t:])


def _mod_kernel(c_ref, w_ref, b_ref, o_ref):
    c = c_ref[...]
    act = (c * jax.nn.sigmoid(c)).astype(BF16)
    o_ref[...] = _dot(act, w_ref[...].astype(BF16)) + b_ref[...]


def _modulation(cc, w_ada, b_ada):
    L, D, N = w_ada.shape
    R = cc.shape[0]
    tn = 1024
    return pl.pallas_call(
        _mod_kernel,
        grid=(L, N // tn),
        in_specs=[
            pl.BlockSpec((R, D), lambda l, j: (0, 0)),
            pl.BlockSpec((None, D, tn), lambda l, j: (l, 0, j)),
            pl.BlockSpec((None, 1, tn), lambda l, j: (l, 0, j)),
        ],
        out_specs=pl.BlockSpec((None, R, tn), lambda l, j: (l, 0, j)),
        out_shape=jax.ShapeDtypeStruct((L, R, N), F32),
        compiler_params=_cparams(("parallel", "parallel")),
        name="adaln_mod",
    )(cc, w_ada, b_ada.reshape(L, 1, N))


def _modnorm_into(dst_ref, x_ref, g_ref, sh_ref, sc_ref, chunk=128):
    tm = x_ref.shape[0]
    sh = sh_ref[...]
    gain = g_ref[...] * (1.0 + sc_ref[...])

    def body(r, carry):
        rows = pl.ds(pl.multiple_of(r * chunk, chunk), chunk)
        xf = x_ref[rows, :]
        ms = jnp.mean(xf * xf, axis=-1, keepdims=True)
        dst_ref[rows, :] = (xf * lax.rsqrt(ms + NORM_EPS) * gain + sh).astype(dst_ref.dtype)
        return carry

    lax.fori_loop(0, tm // chunk, body, 0)


def _proj_kernel(x_ref, g_ref, sh_ref, sc_ref, w_ref, pv_ref, cos_ref, sin_ref, bd_ref,
                 qkv_ref, du_ref, h_ref, y_ref, *, plain_steps):
    j = pl.program_id(1)

    @pl.when(j == 0)
    def _():
        _modnorm_into(h_ref, x_ref, g_ref, sh_ref, sc_ref)
        y_ref[...] = jnp.zeros_like(y_ref)

    def norm_rope(yp):
        gain = pv_ref[0:1, :]
        normed = pv_ref[1:2, :] > 0.0
        roped = pv_ref[2:3, :] > 0.0
        y2 = (yp * yp).astype(BF16)
        hw = bd_ref.shape[0]
        ss = jnp.concatenate([_dot(y2[:, c * hw:(c + 1) * hw], bd_ref[...]) for c in range(y2.shape[1] // hw)],
                             axis=1)
        rs = lax.rsqrt(ss * (1.0 / HEAD_DIM) + NORM_EPS)
        n = yp * (jnp.where(normed, rs, 1.0) * gain)
        cos = jnp.where(roped, cos_ref[...], 1.0)
        sin = jnp.where(roped, sin_ref[...], 0.0)
        w = n.shape[-1]
        lane = lax.broadcasted_iota(jnp.int32, n.shape, 1)
        partner = jnp.where((lane % 32) < 16, pltpu.roll(n, w - 16, 1), pltpu.roll(n, 16, 1))
        return n * cos + partner * sin

    def step(epilogue):
        qkv_ref[...] = epilogue(y_ref[...]).astype(BF16)
        y = _dot(h_ref[...], w_ref[...])
        y_ref[...] = y

        @pl.when(j == pl.num_programs(1) - 1)
        def _():
            du_ref[...] = y

    prev_plain = functools.reduce(lambda a, b: a | b, [j == jp for jp in plain_steps])

    @pl.when(prev_plain)
    def _():
        step(lambda yp: yp)

    @pl.when(jnp.logical_not(prev_plain))
    def _():
        step(norm_rope)


_PLAIN_QKV_BLOCKS = (2, 5)


def _project(x2, mod5, l, row_of_tile, g_norm, w_cat, pvecs, cos_t, sin_t, table_block, bd, tm):
    M, D = x2.shape
    nblk = w_cat.shape[2] // MIX_W
    nq = nblk - 1
    mod_spec = lambda k: pl.BlockSpec((None, None, None, 1, D), lambda i, j: (l, row_of_tile(i), k, 0, 0))
    plain_steps = (0,) + tuple(b + 1 for b in _PLAIN_QKV_BLOCKS)
    return pl.pallas_call(
        functools.partial(_proj_kernel, plain_steps=plain_steps),
        grid=(M // tm, nblk),
        in_specs=[
            pl.BlockSpec((tm, D), lambda i, j: (i, 0)),
            pl.BlockSpec((1, D), lambda i, j: (0, 0)),
            mod_spec(0), mod_spec(1),
            pl.BlockSpec((None, D, MIX_W), lambda i, j: (l, 0, j)),
            pl.BlockSpec((None, 8, MIX_W), lambda i, j: (jnp.maximum(j - 1, 0), 0, 0)),
            pl.BlockSpec((tm, MIX_W), lambda i, j: (table_block(i), 0)),
            pl.BlockSpec((tm, MIX_W), lambda i, j: (table_block(i), 0)),
            pl.BlockSpec(bd.shape, lambda i, j: (0, 0)),
        ],
        out_specs=[
            pl.BlockSpec((tm, MIX_W), lambda i, j: (i, jnp.maximum(j - 1, 0))),
            pl.BlockSpec((tm, MIX_W), lambda i, j: (i, 0)),
            pl.BlockSpec((tm, D), lambda i, j: (i, 0)),
        ],
        out_shape=[
            jax.ShapeDtypeStruct((M, nq * MIX_W), BF16),
            jax.ShapeDtypeStruct((M, MIX_W), F32),
            jax.ShapeDtypeStruct((M, D), BF16),
        ],
        scratch_shapes=[pltpu.VMEM((tm, MIX_W), F32)],
        compiler_params=_cparams(("parallel", "arbitrary")),
        name="in_proj",
    )(x2, g_norm, mod5, mod5, w_cat, pvecs, cos_t, sin_t, bd)


_QA, _KA, _VA = 0, 4, 8
_QB, _KB, _VB = 12, 16, 20
_QC = 24
_KC, _VC = 28, 30


def _softmax_pv(parts, extra_logit=None):
    s = jnp.concatenate([s for s, _ in parts], axis=1) if len(parts) > 1 else parts[0][0]
    m = s.max(axis=-1, keepdims=True)
    if extra_logit is not None:
        m = jnp.maximum(m, extra_logit)
    p = jnp.exp2(s - m)
    l = p.sum(axis=-1, keepdims=True)
    if extra_logit is not None:
        l = l + jnp.exp2(extra_logit - m)
    pb = p.astype(BF16)
    o = None
    start = 0
    for sp, v in parts:
        width = sp.shape[1]
        term = _dot(pb[:, start:start + width], v)
        o = term if o is None else o + term
        start += width
    return o / l


NA_GROUP = 4
NA_UNION = NA_GROUP + NA_ROWS


def _na_kernel(q_ref, k_ref, v_ref, kc_ref, vc_ref, bias_ref, o_ref, *, rows, units):
    g0 = pl.program_id(2) * units
    n_groups = rows // NA_GROUP
    tq = NA_GROUP * GRID_W
    kc = kc_ref[...]
    vc = vc_ref[...]
    scores = []
    for u in range(units):
        g = g0 + u
        start = jnp.clip(NA_GROUP * g - NA_ROWS // 2, 0, rows - NA_UNION)
        variant = jnp.where(g == 0, 0, jnp.where(g == n_groups - 1, 2, 1))
        krows = pl.ds(pl.multiple_of(start * GRID_W, GRID_W), NA_UNION * GRID_W)
        qs = _stack_halves(q_ref[u * tq:(u + 1) * tq, :])
        s = _dot_nt(qs, k_ref[krows, :]) + bias_ref[variant]
        sc = _dot_nt(qs, kc)
        scores.append((s, sc, krows))
    for u, (s, sc, krows) in enumerate(scores):
        o = _softmax_pv([(s, v_ref[krows, :]), (sc, vc)])
        o_ref[u * tq:(u + 1) * tq, :] = _merge_halves(o, tq).astype(o_ref.dtype)


def _na_attention(qkv, qkv_c, bias, B, S, Lc):
    rows = S // GRID_W
    assert rows % NA_GROUP == 0 and rows >= NA_UNION
    n_groups = rows // NA_GROUP
    units = 8 if n_groups % 8 == 0 else (4 if n_groups % 4 == 0 else 1)
    nstep = n_groups // units
    tq = units * NA_GROUP * GRID_W
    npair = MIX_W // LANES
    return pl.pallas_call(
        functools.partial(_na_kernel, rows=rows, units=units),
        grid=(B, npair, nstep),
        in_specs=[
            pl.BlockSpec((tq, LANES), lambda b, p, r: (b * nstep + r, _QA + p)),
            pl.BlockSpec((S, LANES), lambda b, p, r: (b, _KA + p)),
            pl.BlockSpec((S, LANES), lambda b, p, r: (b, _VA + p)),
            pl.BlockSpec((Lc, LANES), lambda b, p, r: (b, _KA + p)),
            pl.BlockSpec((Lc, LANES), lambda b, p, r: (b, _VA + p)),
            pl.BlockSpec((None,) + bias.shape[1:], lambda b, p, r: (p, 0, 0, 0)),
        ],
        out_specs=pl.BlockSpec((tq, LANES), lambda b, p, r: (b * nstep + r, p)),
        out_shape=jax.ShapeDtypeStruct((B * S, MIX_W), BF16),
        compiler_params=_cparams(("parallel", "parallel", "arbitrary")),
        name="na_attention",
    )(qkv, qkv, qkv, qkv_c, qkv_c, bias)


def _na_bias_kernel(rpb_ref, onehot_ref, valid_ref, o_ref):
    r = rpb_ref[...]
    hi = r.astype(BF16)
    r1 = r - hi.astype(F32)
    mid = r1.astype(BF16)
    lo = (r1 - mid.astype(F32)).astype(BF16)
    oh = onehot_ref[...]
    t = _dot(hi, oh) + _dot(mid, oh) + _dot(lo, oh)
    o_ref[...] = jnp.where(valid_ref[...] > 0.0, t * LOG2E, MASK_VALUE)


def _na_window_pattern(rows, g):
    start_u = np.clip(NA_GROUP * g - NA_ROWS // 2, 0, rows - NA_UNION)
    r = NA_GROUP * g + np.arange(NA_GROUP)[:, None]
    start_r = np.clip(r - NA_ROWS // 2, 0, rows - NA_ROWS)
    kr = start_u + np.arange(NA_UNION)[None, :]
    valid = (kr >= start_r) & (kr < start_r + NA_ROWS)
    return np.where(valid, kr - r + (NA_ROWS - 1), 0), valid


def _na_bias_table(rpb, rows):
    H, n_dr, n_dc = rpb.shape
    n_groups = rows // NA_GROUP
    patterns = [_na_window_pattern(rows, g) for g in (0, 1, n_groups - 1)]
    for g in range(1, n_groups - 1):
        dr_g, valid_g = _na_window_pattern(rows, g)
        assert (dr_g == patterns[1][0]).all() and (valid_g == patterns[1][1]).all()
    col = np.arange(GRID_W)
    col_start = np.clip(col - NA_COLS // 2, 0, GRID_W - NA_COLS)
    valid = (col[None, :] >= col_start[:, None]) & (col[None, :] < col_start[:, None] + NA_COLS)
    dc = np.clip(col[None, :] - col[:, None], -(NA_COLS - 1), NA_COLS - 1) + (NA_COLS - 1)
    onehot = (np.arange(LANES)[:, None] == dc.reshape(1, -1)).astype(np.float32)
    n_rows = -(-(H * n_dr) // LANES) * LANES
    rpb_rows = jnp.zeros((n_rows, LANES), F32).at[:H * n_dr, :n_dc].set(rpb.reshape(H * n_dr, n_dc).astype(F32))
    t = pl.pallas_call(
        _na_bias_kernel,
        out_shape=jax.ShapeDtypeStruct((n_rows, GRID_W * GRID_W), F32),
        name="na_bias",
    )(rpb_rows, jnp.asarray(onehot, BF16), jnp.asarray(valid.reshape(1, -1), F32))
    t3 = t.reshape(n_rows, GRID_W, GRID_W)
    out_rows, out_cols = 2 * NA_GROUP * GRID_W, NA_UNION * GRID_W
    return pl.pallas_call(
        functools.partial(_na_assemble_kernel, patterns=patterns, n_dr=n_dr),
        grid=(H // 2,),
        in_specs=[pl.BlockSpec(t3.shape, lambda p: (0, 0, 0))],
        out_specs=pl.BlockSpec((None, len(patterns), out_rows, out_cols), lambda p: (p, 0, 0, 0)),
        out_shape=jax.ShapeDtypeStruct((H // 2, len(patterns), out_rows, out_cols), F32),
        compiler_params=_cparams(("parallel",)),
        name="na_bias_assemble",
    )(t3)


def _na_assemble_kernel(t3_ref, o_ref, *, patterns, n_dr):
    p = pl.program_id(0)
    masked = jnp.full((GRID_W, GRID_W), MASK_VALUE, F32)
    for v, (dr, ok) in enumerate(patterns):
        for parity in range(2):
            base = (2 * p + parity) * n_dr
            for rr in range(NA_GROUP):
                r0 = (parity * NA_GROUP + rr) * GRID_W
                for j in range(NA_UNION):
                    tile = t3_ref[base + int(dr[rr, j])] if ok[rr, j] else masked
                    o_ref[v, r0:r0 + GRID_W, j * GRID_W:(j + 1) * GRID_W] = tile


def _lambda_full(lamp_ref, lam_init):
    lp = lamp_ref[...]
    a = jnp.sum(lp[0:1] * lp[1:2], axis=-1, keepdims=True)
    b = jnp.sum(lp[2:3] * lp[3:4], axis=-1, keepdims=True)
    return jnp.exp(a) - jnp.exp(b) + lam_init


def _subln(o, g, lam_init):
    ms = jnp.mean(o * o, axis=-1, keepdims=True)
    return o * lax.rsqrt(ms + NORM_EPS) * g * (1.0 - lam_init)


def _fold_lanes(x, op):
    out = x[:, :LANES]
    for b in range(1, x.shape[1] // LANES):
        out = op(out, x[:, b * LANES:(b + 1) * LANES])
    return out


def _diff_kernel(q_ref, k_ref, v_ref, kc_ref, vc_ref, lamp_ref, g_ref, o_ref, s_ref, *, lam_init, tk):
    tq = q_ref.shape[0]
    S = k_ref.shape[0]
    nkb = S // tk
    lam = _lambda_full(lamp_ref, lam_init)
    nh = q_ref.shape[1] // LANES
    heads = [slice(h * LANES, (h + 1) * LANES) for h in range(nh)]
    qs = [_stack_halves(q_ref[:, hs]) for hs in heads]
    lane_max, m, lane_sum, acc = [None] * nh, [None] * nh, [None] * nh, [None] * nh

    def pass1_block(h, kb):
        s = _dot_nt(qs[h], k_ref[kb * tk:(kb + 1) * tk, heads[h]])
        s_ref[h, kb] = s
        blk = _fold_lanes(s, jnp.maximum)
        lane_max[h] = blk if lane_max[h] is None else jnp.maximum(lane_max[h], blk)

    def pass1_finish(h):
        sc = _dot_nt(qs[h], kc_ref[:, heads[h]])
        m[h] = jnp.maximum(lane_max[h], _fold_lanes(sc, jnp.maximum)).max(axis=-1, keepdims=True)
        p = jnp.exp2(sc - m[h])
        lane_sum[h] = _fold_lanes(p, jnp.add)
        acc[h] = _dot(p.astype(BF16), vc_ref[:, heads[h]])

    def pass2_block(h, kb):
        p = jnp.exp2(s_ref[h, kb] - m[h])
        lane_sum[h] = lane_sum[h] + _fold_lanes(p, jnp.add)
        acc[h] = acc[h] + _dot(p.astype(BF16), v_ref[kb * tk:(kb + 1) * tk, heads[h]])

    def pass2_finish(h):
        on = acc[h] / lane_sum[h].sum(axis=-1, keepdims=True)
        o = on[:tq] - lam * on[tq:]
        o_ref[:, heads[h]] = _subln(o, g_ref[...], lam_init).astype(o_ref.dtype)

    for t in range(nh + 1):
        for kb in range(nkb):
            if t < nh:
                pass1_block(t, kb)
            if t >= 1:
                pass2_block(t - 1, kb)
        if t < nh:
            pass1_finish(t)
        if t >= 1:
            pass2_finish(t - 1)


def _diff_attention(qkv, qkv_c, lamp, subln_g, lam_init, B, S, Lc):
    tq = 256
    nq = S // tq
    hw = 4 * LANES
    ngrp = MIX_W // hw
    qb, kb, vb = _QB * LANES // hw, _KB * LANES // hw, _VB * LANES // hw
    tk = 512
    nh = hw // LANES
    return pl.pallas_call(
        functools.partial(_diff_kernel, lam_init=lam_init, tk=tk),
        grid=(B, ngrp, nq),
        in_specs=[
            pl.BlockSpec((tq, hw), lambda b, h, i: (b * nq + i, qb + h)),
            pl.BlockSpec((S, hw), lambda b, h, i: (b, kb + h)),
            pl.BlockSpec((S, hw), lambda b, h, i: (b, vb + h)),
            pl.BlockSpec((Lc, hw), lambda b, h, i: (b, kb + h)),
            pl.BlockSpec((Lc, hw), lambda b, h, i: (b, vb + h)),
            pl.BlockSpec((4, HEAD_DIM), lambda b, h, i: (0, 0)),
            pl.BlockSpec((1, LANES), lambda b, h, i: (0, 0)),
        ],
        out_specs=pl.BlockSpec((tq, hw), lambda b, h, i: (b * nq + i, h)),
        out_shape=jax.ShapeDtypeStruct((B * S, MIX_W), BF16),
        scratch_shapes=[pltpu.VMEM((nh, S // tk, 2 * tq, tk), F32)],
        compiler_params=_cparams(("parallel", "parallel", "arbitrary")),
        name="diff_attention",
    )(qkv, qkv, qkv, qkv_c, qkv_c, lamp, subln_g)


def _stack_group(q4):
    return jnp.concatenate([_stack_halves(q4[:, :LANES]), _stack_halves(q4[:, LANES:])], axis=0)


def _merge_group(o, t):
    return jnp.concatenate([_merge_halves(o[:2 * t], t), _merge_halves(o[2 * t:], t)], axis=-1)


def _sink_column(sink_ref, c, t):
    group = sink_ref.shape[0] // SWA_KV_HEADS
    return jnp.concatenate([jnp.full((t, 1), sink_ref[c * group + g] * LOG2E, F32) for g in range(group)], axis=0)


def _swa_kernel(sink_ref, q_ref, k_ref, v_ref, kc_ref, vc_ref, mask_ref, o_ref, *, qb):
    c = pl.program_id(1)
    n0 = pl.program_id(2) * qb
    W = SWA_WINDOW
    S = k_ref.shape[0]
    nb = S // W
    kc = kc_ref[...]
    vc = vc_ref[...]
    sink_col = _sink_column(sink_ref, c, W)
    group = sink_ref.shape[0] // SWA_KV_HEADS
    scores = []
    for t in range(qb):
        n = n0 + t
        ks = pl.multiple_of(jnp.clip((n - 1) * W, 0, S - 3 * W), W)
        krows = pl.ds(ks, 3 * W)
        variant = jnp.where(n == 0, 0, jnp.where(n == nb - 1, 2, 1))
        mask = mask_ref[variant]
        qs = _stack_group(q_ref[t * W:(t + 1) * W, :])
        s = _dot_nt(qs, k_ref[krows, :]) + jnp.concatenate([mask] * group, axis=0)
        sc = _dot_nt(qs, kc)
        scores.append((s, sc, krows))
    for t, (s, sc, krows) in enumerate(scores):
        o = _softmax_pv([(s, v_ref[krows, :]), (sc, vc)], extra_logit=sink_col)
        o_ref[t * W:(t + 1) * W, :] = _merge_group(o, W).astype(o_ref.dtype)


def _swa_mask_table():
    W = SWA_WINDOW
    i = np.arange(W)[:, None]
    j = np.arange(3 * W)[None, :]
    tabs = [np.where(np.abs(j - shift - i) <= W, 0.0, MASK_VALUE) for shift in (0, W, 2 * W)]
    return jnp.asarray(np.stack(tabs), F32)


def _swa_attention(qkv, qkv_c, sink, mask, B, S, Lc):
    W = SWA_WINDOW
    nb = S // W
    qb = 16 if nb % 16 == 0 else (8 if nb % 8 == 0 else 1)
    nqb = nb // qb
    return pl.pallas_call(
        functools.partial(_swa_kernel, qb=qb),
        grid=(B, SWA_KV_HEADS, nqb),
        in_specs=[
            pl.BlockSpec(memory_space=pltpu.SMEM),
            pl.BlockSpec((qb * W, 2 * LANES), lambda b, c, n: (b * nqb + n, _QC // 2 + c)),
            pl.BlockSpec((S, LANES), lambda b, c, n: (b, _KC + c)),
            pl.BlockSpec((S, LANES), lambda b, c, n: (b, _VC + c)),
            pl.BlockSpec((Lc, LANES), lambda b, c, n: (b, _KC + c)),
            pl.BlockSpec((Lc, LANES), lambda b, c, n: (b, _VC + c)),
            pl.BlockSpec(mask.shape, lambda b, c, n: (0, 0, 0)),
        ],
        out_specs=pl.BlockSpec((qb * W, 2 * LANES), lambda b, c, n: (b * nqb + n, c)),
        out_shape=jax.ShapeDtypeStruct((B * S, MIX_W), BF16),
        compiler_params=_cparams(("parallel", "parallel", "arbitrary")),
        name="swa_attention",
    )(sink, qkv, qkv, qkv, qkv_c, qkv_c, mask)


def _ctx_attn_kernel(sink_ref, qkv_ref, lamp_ref, g_ref, oa_ref, ob_ref, os_ref, *, lam_init):
    Lc = qkv_ref.shape[0]

    def cols(blk, width=LANES):
        return qkv_ref[:, blk * LANES: blk * LANES + width]

    for p in range(MIX_W // LANES):
        qs = _stack_halves(cols(_QA + p))
        o = _softmax_pv([(_dot_nt(qs, cols(_KA + p)), cols(_VA + p))])
        oa_ref[:, p * LANES:(p + 1) * LANES] = _merge_halves(o, Lc).astype(oa_ref.dtype)

    lam = _lambda_full(lamp_ref, lam_init)
    for h in range(MIX_W // LANES):
        qs = _stack_halves(cols(_QB + h))
        on = _softmax_pv([(_dot_nt(qs, cols(_KB + h)), cols(_VB + h))])
        o = on[:Lc] - lam * on[Lc:]
        ob_ref[:, h * LANES:(h + 1) * LANES] = _subln(o, g_ref[...], lam_init).astype(ob_ref.dtype)

    for c in range(SWA_KV_HEADS):
        qs = _stack_group(cols(_QC + 2 * c, 2 * LANES))
        o = _softmax_pv([(_dot_nt(qs, cols(_KC + c)), cols(_VC + c))],
                        extra_logit=_sink_column(sink_ref, c, Lc))
        os_ref[:, c * 2 * LANES:(c + 1) * 2 * LANES] = _merge_group(o, Lc).astype(os_ref.dtype)


def _ctx_attention(qkv_c, sink, lamp, subln_g, lam_init, B, Lc):
    NQ = qkv_c.shape[1]
    out = jax.ShapeDtypeStruct((B * Lc, MIX_W), BF16)
    ospec = pl.BlockSpec((Lc, MIX_W), lambda b: (b, 0))
    return pl.pallas_call(
        functools.partial(_ctx_attn_kernel, lam_init=lam_init),
        grid=(B,),
        in_specs=[
            pl.BlockSpec(memory_space=pltpu.SMEM),
            pl.BlockSpec((Lc, NQ), lambda b: (b, 0)),
            pl.BlockSpec((4, HEAD_DIM), lambda b: (0, 0)),
            pl.BlockSpec((1, LANES), lambda b: (0, 0)),
        ],
        out_specs=[ospec, ospec, ospec],
        out_shape=[out, out, out],
        compiler_params=_cparams(("parallel",)),
        name="ctx_attention",
    )(sink, qkv_c, lamp, subln_g)


def _shift_down(x, k, t_idx):
    return jnp.where(t_idx >= k, pltpu.roll(x, k, 0), 0.0)


def _shift_up(x, k, t_idx):
    T = x.shape[0]
    return jnp.where(t_idx < T - k, pltpu.roll(x, T - k, 0), 0.0)


def _pool_kernel(u_ref, w_ref, scale_ref, o_ref):
    T = u_ref.shape[0]
    pg = u_ref.shape[1] // len(POOL_WINDOWS)
    t_idx = lax.broadcasted_iota(jnp.int32, (T, pg), 0)
    for g, w in enumerate(POOL_WINDOWS):
        u = u_ref[:, g * pg:(g + 1) * pg]
        back = u
        fwd = u
        span = 1
        while span < w // 2:
            back = back + _shift_down(back, span, t_idx)
            fwd = fwd + _shift_up(fwd, span, t_idx)
            span *= 2
        win = _shift_down(back, 1, t_idx) + fwd
        lo = jnp.clip(t_idx - w // 2, 0, T)
        hi = jnp.clip(t_idx - w // 2 + w, 0, T)
        d = (win / (hi - lo).astype(F32) - u).astype(BF16)
        mixed = _dot(d, w_ref[g])
        o_ref[:, g * pg:(g + 1) * pg] = (mixed * scale_ref[:, g * pg:(g + 1) * pg]).astype(o_ref.dtype)


def _pool_mixer(du, w_grp, scale, nseq, T):
    return pl.pallas_call(
        _pool_kernel,
        grid=(nseq,),
        in_specs=[
            pl.BlockSpec((T, MIX_W), lambda b: (b, 0)),
            pl.BlockSpec(w_grp.shape, lambda b: (0, 0, 0)),
            pl.BlockSpec((1, MIX_W), lambda b: (0, 0)),
        ],
        out_specs=pl.BlockSpec((T, MIX_W), lambda b: (b, 0)),
        out_shape=jax.ShapeDtypeStruct((nseq * T, MIX_W), BF16),
        compiler_params=_cparams(("parallel",)),
        name="pool_mixer",
    )(du, w_grp, scale)


def _merge_kernel(h_ref, oa_ref, ob_ref, os_ref, od_ref, wg0_ref, wg1_ref, wg2_ref, wg3_ref, wb_ref,
                  acc_ref, f32_ref):
    h = h_ref[...]
    branches = zip((oa_ref, ob_ref, os_ref, od_ref), (wg0_ref, wg1_ref, wg2_ref, wg3_ref))
    for n, (o_ref, wg_ref) in enumerate(branches):
        gate = jax.nn.sigmoid(_dot(h, wg_ref[...]))
        term = gate * _dot(o_ref[...], wb_ref[n])
        if n == 0:
            f32_ref[...] = term
        else:
            f32_ref[...] += term
    acc_ref[...] = f32_ref[...].astype(acc_ref.dtype)


def _merge(h, outs, l, w_gate, w_br, tm, tn):
    M, D = h.shape
    nj = D // tn
    ospec = pl.BlockSpec((tm, MIX_W), lambda i, j: (i, 0))
    gate_spec = lambda n: pl.BlockSpec((None, D, tn), lambda i, j: (l, 0, n * nj + j))
    return pl.pallas_call(
        _merge_kernel,
        grid=(M // tm, nj),
        in_specs=[
            pl.BlockSpec((tm, D), lambda i, j: (i, 0)),
            ospec, ospec, ospec, ospec,
            gate_spec(0), gate_spec(1), gate_spec(2), gate_spec(3),
            pl.BlockSpec((None, N_BRANCH, MIX_W, tn), lambda i, j: (l, 0, 0, j)),
        ],
        out_specs=pl.BlockSpec((tm, tn), lambda i, j: (i, j)),
        out_shape=jax.ShapeDtypeStruct((M, D), BF16),
        scratch_shapes=[pltpu.VMEM((tm, tn), F32)],
        compiler_params=_cparams(("parallel", "arbitrary")),
        name="gated_merge",
    )(h, *outs, w_gate, w_gate, w_gate, w_gate, w_br)


def _outproj_kernel(a_ref, w_ref, x_ref, g_ref, o_ref, *, tc):
    a = a_ref[...]
    for c in range(o_ref.shape[1] // tc):
        cs = slice(c * tc, (c + 1) * tc)
        o_ref[:, cs] = x_ref[:, cs] + g_ref[:, cs] * _dot(a, w_ref[:, cs])


def _out_project(acc, l, w_o, x2, mod5, row_of_tile, tm):
    M, D = x2.shape
    return pl.pallas_call(
        functools.partial(_outproj_kernel, tc=512),
        grid=(M // tm,),
        in_specs=[
            pl.BlockSpec((tm, D), lambda i: (i, 0)),
            pl.BlockSpec((None, D, D), lambda i: (l, 0, 0), pipeline_mode=pl.Buffered(1)),
            pl.BlockSpec((tm, D), lambda i: (i, 0)),
            pl.BlockSpec((None, None, None, 1, D), lambda i: (l, row_of_tile(i), 2, 0, 0)),
        ],
        out_specs=pl.BlockSpec((tm, D), lambda i: (i, 0)),
        out_shape=jax.ShapeDtypeStruct((M, D), F32),
        compiler_params=_cparams(("parallel",)),
        name="out_proj",
    )(acc, w_o, x2, mod5)


def _ffn_kernel(x_ref, g_ref, sh_ref, sc_ref, gate_ref, wg_ref, wu_ref, wd_ref, o_ref, h_ref, *, tc, nf):
    f = pl.program_id(1)
    D = o_ref.shape[1]

    last = nf - 1

    @pl.when(f == 0)
    def _():
        _modnorm_into(h_ref, x_ref, g_ref, sh_ref, sc_ref)

    def step(first, final):
        h = h_ref[...]
        a = _dot(h, wg_ref[...])
        u = _dot(h, wu_ref[...])
        act = (a * jax.nn.sigmoid(a) * u).astype(BF16)
        for c in range(D // tc):
            cs = slice(c * tc, (c + 1) * tc)
            part = _dot(act, wd_ref[:, cs])
            total = part if first else o_ref[:, cs] + part
            o_ref[:, cs] = x_ref[:, cs] + gate_ref[:, cs] * total if final else total

    if nf == 1:
        step(True, True)
        return

    @pl.when(f == 0)
    def _():
        step(True, False)

    @pl.when((f > 0) & (f < last))
    def _():
        step(False, False)

    @pl.when(f == last)
    def _():
        step(False, True)


def _ffn(x2, mod5, l, row_of_tile, g_norm, wg, wu, wd, tm, tf):
    M, D = x2.shape
    F = wg.shape[2]
    mod_spec = lambda k: pl.BlockSpec((None, None, None, 1, D), lambda i, f: (l, row_of_tile(i), k, 0, 0))
    return pl.pallas_call(
        functools.partial(_ffn_kernel, tc=512, nf=F // tf),
        grid=(M // tm, F // tf),
        in_specs=[
            pl.BlockSpec((tm, D), lambda i, f: (i, 0)),
            pl.BlockSpec((1, D), lambda i, f: (0, 0)),
            mod_spec(3), mod_spec(4), mod_spec(5),
            pl.BlockSpec((None, D, tf), lambda i, f: (l, 0, f)),
            pl.BlockSpec((None, D, tf), lambda i, f: (l, 0, f)),
            pl.BlockSpec((None, tf, D), lambda i, f: (l, f, 0)),
        ],
        out_specs=pl.BlockSpec((tm, D), lambda i, f: (i, 0)),
        out_shape=jax.ShapeDtypeStruct((M, D), F32),
        scratch_shapes=[pltpu.VMEM((tm, D), BF16)],
        compiler_params=_cparams(("parallel", "arbitrary")),
        name="swiglu_ffn",
    )(x2, g_norm, mod5, mod5, mod5, wg, wu, wd)


def _rope_tables(S):
    t = jnp.arange(S)
    pos = jnp.stack([t // GRID_W, t % GRID_W], axis=-1).astype(F32)
    n_freq = HEAD_DIM // 4
    inv = ROPE_BASE ** (-jnp.arange(n_freq, dtype=F32) / n_freq)
    ang = pos[:, :, None] * inv
    cos, sin = jnp.cos(ang), jnp.sin(ang)
    cos_h = jnp.concatenate([cos, cos], axis=-1).reshape(S, HEAD_DIM)
    sin_h = jnp.concatenate([-sin, sin], axis=-1).reshape(S, HEAD_DIM)
    reps = MIX_W // HEAD_DIM
    return jnp.tile(cos_h, (1, reps)), jnp.tile(sin_h, (1, reps))


def _split_w_in(w_in):
    sizes = (MIX_W,) * 7 + (SWA_KV_HEADS * HEAD_DIM,) * 2 + (MIX_W,)
    offs = [0]
    for s in sizes:
        offs.append(offs[-1] + s)
    parts = [w_in[..., offs[i]:offs[i + 1]] for i in range(len(sizes))]
    return parts, w_in[..., offs[-1]:]


def _dup_heads(w):
    lead = w.shape[:-1]
    w4 = jnp.repeat(w.reshape(lead + (SWA_KV_HEADS, 1, HEAD_DIM)), 2, axis=-2)
    return w4.reshape(lead + (2 * SWA_KV_HEADS * HEAD_DIM,))


def _tile_heads(g, scale=1.0):
    return jnp.tile(g.astype(F32) * scale, MIX_W // HEAD_DIM)


def _proj_column_vectors(gains):
    rope_blocks = (3, 4, 6, 7)
    half = MIX_W // 2
    rows = []
    for j, g in enumerate(gains):
        gain = jnp.ones((MIX_W,), F32) if g is None else g
        normed = jnp.zeros((MIX_W,), F32) if g is None else jnp.ones((MIX_W,), F32)
        roped = jnp.ones((MIX_W,), F32) if j in rope_blocks else jnp.zeros((MIX_W,), F32)
        if j == 7:
            keep = (jnp.arange(MIX_W) < half).astype(F32)
            gain = jnp.where(keep > 0, gain, 1.0)
            normed, roped = normed * keep, roped * keep
        rows.append(jnp.stack([gain, normed, roped] + [jnp.zeros((MIX_W,), F32)] * 5))
    return jnp.stack(rows)


def _pick_tile(M, unit, target):
    t = min(target, M)
    while M % t or (t > unit and t % unit) or (t < unit and unit % t):
        t //= 2
    return t


def kernel(x, c, ctx, c_ctx, w_ada, b_ada, norm_mix, norm_ffn, w_in, a_q_norm, a_k_norm, a_rpb, b_q_norm, b_k_norm, b_lam_q1, b_lam_k1, b_lam_q2, b_lam_k2, b_subln, c_q_norm, c_k_norm, c_sink, d_w, d_scale, w_branch, w_out, w_ffn_gate, w_ffn_up, w_ffn_down):
    B, S, D = x.shape
    Lc = ctx.shape[1]
    L = w_ada.shape[0]
    assert D == N_BRANCH * MIX_W and S % GRID_W == 0 and S // GRID_W >= NA_ROWS and S >= 3 * SWA_WINDOW

    tm = _pick_tile(S, S, 512)
    tm_proj = _pick_tile(S, S, 1024)
    tm_ffn = _pick_tile(S, S, 1024)
    lat_row_ffn = lambda i: i // (S // tm_ffn)
    tm_merge = _pick_tile(B * S, 8, 1024)
    tmc = _pick_tile(B * Lc, 8, 512)
    lat_row = lambda i: i // (S // tm)
    lat_row_proj = lambda i: i // (S // tm_proj)
    ctx_row = lambda i: B
    n_mod_rows = -(-(B + 1) // 8) * 8

    cc = jnp.zeros((n_mod_rows, D), F32).at[:B].set(c).at[B].set(c_ctx)
    mod = _modulation(cc, w_ada, b_ada).reshape(L, n_mod_rows, 6, 1, D)

    cos_t, sin_t = _rope_tables(S)
    cos_id = jnp.ones((tmc, MIX_W), F32)
    sin_id = jnp.zeros((tmc, MIX_W), F32)
    lane = jnp.arange(2 * LANES)
    bd = (lane[:, None] // HEAD_DIM == lane[None, :] // HEAD_DIM).astype(BF16)
    swa_mask = _swa_mask_table()

    x2 = x.reshape(B * S, D)
    c2 = ctx.reshape(B * Lc, D)
    tn_merge = 512

    (aq, ak, av, bq, bk, bv, sq, sk, sv, du), w_gl = _split_w_in(w_in)
    w_cat = jnp.concatenate([aq, ak, av, bq, bk, bv, sq, _dup_heads(sk), _dup_heads(sv), du], axis=-1).astype(BF16)
    w_gate_r = w_gl.astype(BF16)
    w_br = w_branch.astype(BF16)
    w_o = w_out.astype(BF16)
    wg, wu, wd = w_ffn_gate.astype(BF16), w_ffn_up.astype(BF16), w_ffn_down.astype(BF16)

    for l in range(L):
        last = l == L - 1
        lam_init = 0.8 - 0.6 * math.exp(-0.3 * l)

        pvecs = _proj_column_vectors([
            _tile_heads(a_q_norm[l], Q_SCALE), _tile_heads(a_k_norm[l]), None,
            _tile_heads(b_q_norm[l], Q_SCALE), _tile_heads(b_k_norm[l]), None,
            _tile_heads(c_q_norm[l], Q_SCALE),
            _tile_heads(c_k_norm[l]),
        ])
        g_mix_norm = norm_mix[l].reshape(1, D)
        g_ffn_norm = norm_ffn[l].reshape(1, D)
        lamp = jnp.stack([b_lam_q1[l], b_lam_k1[l], b_lam_q2[l], b_lam_k2[l]]).astype(F32)
        subln_g = b_subln[l].reshape(1, LANES).astype(F32)
        sink = c_sink[l].astype(F32)
        w_pool = d_w[l].astype(BF16)
        pool_scale = d_scale[l].reshape(1, MIX_W).astype(F32)

        qkv, du_l, h_l = _project(x2, mod, l, lat_row_proj, g_mix_norm, w_cat, pvecs, cos_t, sin_t,
                                  lambda i: i % (S // tm_proj), bd, tm_proj)
        qkv_c, du_c, h_c = _project(c2, mod, l, ctx_row, g_mix_norm, w_cat, pvecs, cos_id, sin_id,
                                    lambda i: 0, bd, tmc)

        o_a = _na_attention(qkv, qkv_c, _na_bias_table(a_rpb[l], S // GRID_W), B, S, Lc)
        o_b = _diff_attention(qkv, qkv_c, lamp, subln_g, lam_init, B, S, Lc)
        o_s = _swa_attention(qkv, qkv_c, sink, swa_mask, B, S, Lc)
        o_d = _pool_mixer(du_l, w_pool, pool_scale, B, S)
        acc = _merge(h_l, (o_a, o_b, o_s, o_d), l, w_gate_r, w_br, tm_merge, tn_merge)
        x2 = _out_project(acc, l, w_o, x2, mod, lat_row, tm)

        if not last:
            oc_a, oc_b, oc_s = _ctx_attention(qkv_c, sink, lamp, subln_g, lam_init, B, Lc)
            oc_d = _pool_mixer(du_c, w_pool, pool_scale, B, Lc)
            acc_c = _merge(h_c, (oc_a, oc_b, oc_s, oc_d), l, w_gate_r, w_br, tmc, tn_merge)
            c2 = _out_project(acc_c, l, w_o, c2, mod, ctx_row, tmc)
            c2 = _ffn(c2, mod, l, ctx_row, g_ffn_norm, wg, wu, wd, tmc, 512)

        x2 = _ffn(x2, mod, l, lat_row_ffn, g_ffn_norm, wg, wu, wd, tm_ffn, 512)

    return x2.reshape(B, S, D)
```

```python
import functools
import math

import numpy as np
import jax
import jax.numpy as jnp
from jax import lax
from jax.experimental import pallas as pl
from jax.experimental.pallas import tpu as pltpu

F32 = jnp.float32
BF16 = jnp.bfloat16

HEAD_DIM = 64
LANES = 128
GRID_W = 64
N_BRANCH = 4
NA_ROWS = 8
NA_COLS = 16
SWA_WINDOW = 128
SWA_KV_HEADS = 2
POOL_WINDOWS = (2, 4, 8, 16)
ROPE_BASE = 10000.0
NORM_EPS = 1e-6
MASK_VALUE = -1e30
ATTN_SCALE = HEAD_DIM ** -0.5
LOG2E = math.log2(math.e)
Q_SCALE = ATTN_SCALE * LOG2E
MIX_W = 512
VMEM_LIMIT = 56 * 1024 * 1024


def _cparams(sem):
    return pltpu.CompilerParams(dimension_semantics=sem, vmem_limit_bytes=VMEM_LIMIT)


def _dot(a, b):
    return jnp.dot(a, b, preferred_element_type=F32)


def _dot_nt(a, b):
    return lax.dot_general(a, b, (((1,), (1,)), ((), ())), preferred_element_type=F32)


def _lane_lo(shape):
    return lax.broadcasted_iota(jnp.int32, shape, len(shape) - 1) % LANES < HEAD_DIM


def _stack_halves(q2):
    lo = _lane_lo(q2.shape)
    zero = jnp.zeros_like(q2)
    return jnp.concatenate([jnp.where(lo, q2, zero), jnp.where(lo, zero, q2)], axis=0)


def _merge_halves(o, t):
    lo = _lane_lo((t, LANES))
    return jnp.where(lo, o[:t], o[t:])


def _mod_kernel(c_ref, w_ref, b_ref, o_ref):
    c = c_ref[...]
    act = (c * jax.nn.sigmoid(c)).astype(BF16)
    o_ref[...] = _dot(act, w_ref[...].astype(BF16)) + b_ref[...]


def _modulation(cc, w_ada, b_ada):
    L, D, N = w_ada.shape
    R = cc.shape[0]
    tn = 1024
    return pl.pallas_call(
        _mod_kernel,
        grid=(L, N // tn),
        in_specs=[
            pl.BlockSpec((R, D), lambda l, j: (0, 0)),
            pl.BlockSpec((None, D, tn), lambda l, j: (l, 0, j)),
            pl.BlockSpec((None, 1, tn), lambda l, j: (l, 0, j)),
        ],
        out_specs=pl.BlockSpec((None, R, tn), lambda l, j: (l, 0, j)),
        out_shape=jax.ShapeDtypeStruct((L, R, N), F32),
        compiler_params=_cparams(("parallel", "parallel")),
        name="adaln_mod",
    )(cc, w_ada, b_ada.reshape(L, 1, N))


def _modnorm_into(dst_ref, x_ref, g_ref, sh_ref, sc_ref, chunk=128):
    tm = x_ref.shape[0]
    sh = sh_ref[...]
    gain = g_ref[...] * (1.0 + sc_ref[...])

    def body(r, carry):
        rows = pl.ds(pl.multiple_of(r * chunk, chunk), chunk)
        xf = x_ref[rows, :]
        ms = jnp.mean(xf * xf, axis=-1, keepdims=True)
        dst_ref[rows, :] = (xf * lax.rsqrt(ms + NORM_EPS) * gain + sh).astype(dst_ref.dtype)
        return carry

    lax.fori_loop(0, tm // chunk, body, 0)


def _proj_kernel(x_ref, g_ref, sh_ref, sc_ref, w_ref, pv_ref, cos_ref, sin_ref, bd_ref,
                 qkv_ref, du_ref, h_ref, y_ref, *, kinds):
    j = pl.program_id(1)

    def headnorm(yp):
        y2 = (yp * yp).astype(BF16)
        hw = bd_ref.shape[0]
        parts = [_dot(y2[:, c * hw:(c + 1) * hw], bd_ref[...]) for c in range(y2.shape[1] // hw)]
        ss = parts[0] if len(parts) == 1 else jnp.concatenate(parts, axis=1)
        return yp * (lax.rsqrt(ss * (1.0 / HEAD_DIM) + NORM_EPS) * pv_ref[0:1, :yp.shape[1]])

    def rope(n):
        w = n.shape[-1]
        lane = lax.broadcasted_iota(jnp.int32, n.shape, 1)
        partner = jnp.where((lane % 32) < 16, pltpu.roll(n, w - 16, 1), pltpu.roll(n, 16, 1))
        return n * cos_ref[:, :w] + partner * sin_ref[:, :w]

    def half_norm_rope(yp):
        half = yp.shape[1] // 2
        return jnp.concatenate([rope(headnorm(yp[:, :half])), yp[:, half:]], axis=1)

    epilogues = {
        "plain": lambda yp: yp,
        "norm": headnorm,
        "norm_rope": lambda yp: rope(headnorm(yp)),
        "half_norm_rope": half_norm_rope,
    }

    def step(epilogue):
        if epilogue is not None:
            qkv_ref[...] = epilogue(y_ref[...]).astype(BF16)
        y = _dot(h_ref[...], w_ref[...])
        y_ref[...] = y

        @pl.when(j == pl.num_programs(1) - 1)
        def _():
            du_ref[...] = y

    @pl.when(j == 0)
    def _():
        _modnorm_into(h_ref, x_ref, g_ref, sh_ref, sc_ref)
        step(None)

    for kind, epilogue in epilogues.items():
        steps = [b + 1 for b, k in enumerate(kinds) if k == kind]
        if steps:
            pl.when(functools.reduce(lambda a, b: a | b, [j == s for s in steps]))(
                functools.partial(step, epilogue))


_QKV_BLOCK_KINDS = ("norm", "norm", "plain", "norm_rope", "norm_rope", "plain", "norm_rope", "half_norm_rope")


def _project(x2, mod5, l, row_of_tile, g_norm, w_cat, pvecs, cos_t, sin_t, table_block, bd, tm):
    M, D = x2.shape
    nblk = w_cat.shape[2] // MIX_W
    nq = nblk - 1
    mod_spec = lambda k: pl.BlockSpec((None, None, None, 1, D), lambda i, j: (l, row_of_tile(i), k, 0, 0))
    assert len(_QKV_BLOCK_KINDS) == nq
    return pl.pallas_call(
        functools.partial(_proj_kernel, kinds=_QKV_BLOCK_KINDS),
        grid=(M // tm, nblk),
        in_specs=[
            pl.BlockSpec((tm, D), lambda i, j: (i, 0)),
            pl.BlockSpec((1, D), lambda i, j: (0, 0)),
            mod_spec(0), mod_spec(1),
            pl.BlockSpec((None, D, MIX_W), lambda i, j: (l, 0, j)),
            pl.BlockSpec((None, 8, MIX_W), lambda i, j: (jnp.maximum(j - 1, 0), 0, 0)),
            pl.BlockSpec((tm, MIX_W), lambda i, j: (table_block(i), 0)),
            pl.BlockSpec((tm, MIX_W), lambda i, j: (table_block(i), 0)),
            pl.BlockSpec(bd.shape, lambda i, j: (0, 0)),
        ],
        out_specs=[
            pl.BlockSpec((tm, MIX_W), lambda i, j: (i, jnp.maximum(j - 1, 0))),
            pl.BlockSpec((tm, MIX_W), lambda i, j: (i, 0)),
            pl.BlockSpec((tm, D), lambda i, j: (i, 0)),
        ],
        out_shape=[
            jax.ShapeDtypeStruct((M, nq * MIX_W), BF16),
            jax.ShapeDtypeStruct((M, MIX_W), F32),
            jax.ShapeDtypeStruct((M, D), BF16),
        ],
        scratch_shapes=[pltpu.VMEM((tm, MIX_W), F32)],
        compiler_params=_cparams(("parallel", "arbitrary")),
        name="in_proj",
    )(x2, g_norm, mod5, mod5, w_cat, pvecs, cos_t, sin_t, bd)


_QA, _KA, _VA = 0, 4, 8
_QB, _KB, _VB = 12, 16, 20
_QC = 24
_KC, _VC = 28, 30


def _softmax_pv(parts, extra_logit=None):
    s = jnp.concatenate([s for s, _ in parts], axis=1) if len(parts) > 1 else parts[0][0]
    m = s.max(axis=-1, keepdims=True)
    if extra_logit is not None:
        m = jnp.maximum(m, extra_logit)
    p = jnp.exp2(s - m)
    l = p.sum(axis=-1, keepdims=True)
    if extra_logit is not None:
        l = l + jnp.exp2(extra_logit - m)
    pb = p.astype(BF16)
    o = None
    start = 0
    for sp, v in parts:
        width = sp.shape[1]
        term = _dot(pb[:, start:start + width], v)
        o = term if o is None else o + term
        start += width
    return o / l


NA_GROUP = 4
NA_UNION = NA_GROUP + NA_ROWS


def _na_kernel(q_ref, k_ref, v_ref, kc_ref, vc_ref, bias_ref, o_ref, *, rows, units):
    g0 = pl.program_id(2) * units
    n_groups = rows // NA_GROUP
    tq = NA_GROUP * GRID_W
    kc = kc_ref[...]
    vc = vc_ref[...]
    scores = []
    for u in range(units):
        g = g0 + u
        start = jnp.clip(NA_GROUP * g - NA_ROWS // 2, 0, rows - NA_UNION)
        variant = jnp.where(g == 0, 0, jnp.where(g == n_groups - 1, 2, 1))
        krows = pl.ds(pl.multiple_of(start * GRID_W, GRID_W), NA_UNION * GRID_W)
        qs = _stack_halves(q_ref[u * tq:(u + 1) * tq, :])
        s = _dot_nt(qs, k_ref[krows, :]) + bias_ref[variant]
        sc = _dot_nt(qs, kc)
        scores.append((s, sc, krows))
    for u, (s, sc, krows) in enumerate(scores):
        o = _softmax_pv([(s, v_ref[krows, :]), (sc, vc)])
        o_ref[u * tq:(u + 1) * tq, :] = _merge_halves(o, tq).astype(o_ref.dtype)


def _na_attention(qkv, qkv_c, bias, B, S, Lc):
    rows = S // GRID_W
    assert rows % NA_GROUP == 0 and rows >= NA_UNION
    n_groups = rows // NA_GROUP
    units = 8 if n_groups % 8 == 0 else (4 if n_groups % 4 == 0 else 1)
    nstep = n_groups // units
    tq = units * NA_GROUP * GRID_W
    npair = MIX_W // LANES
    return pl.pallas_call(
        functools.partial(_na_kernel, rows=rows, units=units),
        grid=(B, npair, nstep),
        in_specs=[
            pl.BlockSpec((tq, LANES), lambda b, p, r: (b * nstep + r, _QA + p)),
            pl.BlockSpec((S, LANES), lambda b, p, r: (b, _KA + p)),
            pl.BlockSpec((S, LANES), lambda b, p, r: (b, _VA + p)),
            pl.BlockSpec((Lc, LANES), lambda b, p, r: (b, _KA + p)),
            pl.BlockSpec((Lc, LANES), lambda b, p, r: (b, _VA + p)),
            pl.BlockSpec((None,) + bias.shape[1:], lambda b, p, r: (p, 0, 0, 0)),
        ],
        out_specs=pl.BlockSpec((tq, LANES), lambda b, p, r: (b * nstep + r, p)),
        out_shape=jax.ShapeDtypeStruct((B * S, MIX_W), BF16),
        compiler_params=_cparams(("parallel", "parallel", "arbitrary")),
        name="na_attention",
    )(qkv, qkv, qkv, qkv_c, qkv_c, bias)


def _na_bias_kernel(rpb_ref, onehot_ref, valid_ref, o_ref):
    r = rpb_ref[...]
    hi = r.astype(BF16)
    r1 = r - hi.astype(F32)
    mid = r1.astype(BF16)
    lo = (r1 - mid.astype(F32)).astype(BF16)
    oh = onehot_ref[...]
    t = _dot(hi, oh) + _dot(mid, oh) + _dot(lo, oh)
    o_ref[...] = jnp.where(valid_ref[...] > 0.0, t * LOG2E, MASK_VALUE)


def _na_window_pattern(rows, g):
    start_u = np.clip(NA_GROUP * g - NA_ROWS // 2, 0, rows - NA_UNION)
    r = NA_GROUP * g + np.arange(NA_GROUP)[:, None]
    start_r = np.clip(r - NA_ROWS // 2, 0, rows - NA_ROWS)
    kr = start_u + np.arange(NA_UNION)[None, :]
    valid = (kr >= start_r) & (kr < start_r + NA_ROWS)
    return np.where(valid, kr - r + (NA_ROWS - 1), 0), valid


def _na_bias_table(rpb, rows):
    H, n_dr, n_dc = rpb.shape
    n_groups = rows // NA_GROUP
    patterns = [_na_window_pattern(rows, g) for g in (0, 1, n_groups - 1)]
    for g in range(1, n_groups - 1):
        dr_g, valid_g = _na_window_pattern(rows, g)
        assert (dr_g == patterns[1][0]).all() and (valid_g == patterns[1][1]).all()
    col = np.arange(GRID_W)
    col_start = np.clip(col - NA_COLS // 2, 0, GRID_W - NA_COLS)
    valid = (col[None, :] >= col_start[:, None]) & (col[None, :] < col_start[:, None] + NA_COLS)
    dc = np.clip(col[None, :] - col[:, None], -(NA_COLS - 1), NA_COLS - 1) + (NA_COLS - 1)
    onehot = (np.arange(LANES)[:, None] == dc.reshape(1, -1)).astype(np.float32)
    n_rows = -(-(H * n_dr) // LANES) * LANES
    rpb_rows = jnp.zeros((n_rows, LANES), F32).at[:H * n_dr, :n_dc].set(rpb.reshape(H * n_dr, n_dc).astype(F32))
    t = pl.pallas_call(
        _na_bias_kernel,
        out_shape=jax.ShapeDtypeStruct((n_rows, GRID_W * GRID_W), F32),
        name="na_bias",
    )(rpb_rows, jnp.asarray(onehot, BF16), jnp.asarray(valid.reshape(1, -1), F32))
    t3 = t.reshape(n_rows, GRID_W, GRID_W)
    out_rows, out_cols = 2 * NA_GROUP * GRID_W, NA_UNION * GRID_W
    return pl.pallas_call(
        functools.partial(_na_assemble_kernel, patterns=patterns, n_dr=n_dr),
        grid=(H // 2,),
        in_specs=[pl.BlockSpec(t3.shape, lambda p: (0, 0, 0))],
        out_specs=pl.BlockSpec((None, len(patterns), out_rows, out_cols), lambda p: (p, 0, 0, 0)),
        out_shape=jax.ShapeDtypeStruct((H // 2, len(patterns), out_rows, out_cols), F32),
        compiler_params=_cparams(("parallel",)),
        name="na_bias_assemble",
    )(t3)


def _na_assemble_kernel(t3_ref, o_ref, *, patterns, n_dr):
    p = pl.program_id(0)
    masked = jnp.full((GRID_W, GRID_W), MASK_VALUE, F32)
    for v, (dr, ok) in enumerate(patterns):
        for parity in range(2):
            base = (2 * p + parity) * n_dr
            for rr in range(NA_GROUP):
                r0 = (parity * NA_GROUP + rr) * GRID_W
                for j in range(NA_UNION):
                    tile = t3_ref[base + int(dr[rr, j])] if ok[rr, j] else masked
                    o_ref[v, r0:r0 + GRID_W, j * GRID_W:(j + 1) * GRID_W] = tile


def _lambda_full(lamp_ref, lam_init):
    lp = lamp_ref[...]
    a = jnp.sum(lp[0:1] * lp[1:2], axis=-1, keepdims=True)
    b = jnp.sum(lp[2:3] * lp[3:4], axis=-1, keepdims=True)
    return jnp.exp(a) - jnp.exp(b) + lam_init


def _subln(o, g, lam_init):
    ms = jnp.mean(o * o, axis=-1, keepdims=True)
    return o * lax.rsqrt(ms + NORM_EPS) * g * (1.0 - lam_init)


def _fold_lanes(x, op):
    out = x[:, :LANES]
    for b in range(1, x.shape[1] // LANES):
        out = op(out, x[:, b * LANES:(b + 1) * LANES])
    return out


def _diff_kernel(q_ref, k_ref, v_ref, kc_ref, vc_ref, lamp_ref, g_ref, o_ref, s_ref, *, lam_init, tk):
    tq = q_ref.shape[0]
    S = k_ref.shape[0]
    nkb = S // tk
    lam = _lambda_full(lamp_ref, lam_init)
    nh = q_ref.shape[1] // LANES
    heads = [slice(h * LANES, (h + 1) * LANES) for h in range(nh)]
    qs = [_stack_halves(q_ref[:, hs]) for hs in heads]
    lane_max, m, lane_sum, acc = [None] * nh, [None] * nh, [None] * nh, [None] * nh

    def pass1_block(h, kb):
        s = _dot_nt(qs[h], k_ref[kb * tk:(kb + 1) * tk, heads[h]])
        s_ref[h, kb] = s
        blk = _fold_lanes(s, jnp.maximum)
        lane_max[h] = blk if lane_max[h] is None else jnp.maximum(lane_max[h], blk)

    def pass1_finish(h):
        sc = _dot_nt(qs[h], kc_ref[:, heads[h]])
        m[h] = jnp.maximum(lane_max[h], _fold_lanes(sc, jnp.maximum)).max(axis=-1, keepdims=True)
        p = jnp.exp2(sc - m[h])
        lane_sum[h] = _fold_lanes(p, jnp.add)
        acc[h] = _dot(p.astype(BF16), vc_ref[:, heads[h]])

    def pass2_block(h, kb):
        p = jnp.exp2(s_ref[h, kb] - m[h])
        lane_sum[h] = lane_sum[h] + _fold_lanes(p, jnp.add)
        acc[h] = acc[h] + _dot(p.astype(BF16), v_ref[kb * tk:(kb + 1) * tk, heads[h]])

    def pass2_finish(h):
        on = acc[h] / lane_sum[h].sum(axis=-1, keepdims=True)
        o = on[:tq] - lam * on[tq:]
        o_ref[:, heads[h]] = _subln(o, g_ref[...], lam_init).astype(o_ref.dtype)

    for t in range(nh + 1):
        for kb in range(nkb):
            if t < nh:
                pass1_block(t, kb)
            if t >= 1:
                pass2_block(t - 1, kb)
        if t < nh:
            pass1_finish(t)
        if t >= 1:
            pass2_finish(t - 1)


def _diff_attention(qkv, qkv_c, lamp, subln_g, lam_init, B, S, Lc):
    tq = 256
    nq = S // tq
    hw = 4 * LANES
    ngrp = MIX_W // hw
    qb, kb, vb = _QB * LANES // hw, _KB * LANES // hw, _VB * LANES // hw
    tk = 512
    nh = hw // LANES
    return pl.pallas_call(
        functools.partial(_diff_kernel, lam_init=lam_init, tk=tk),
        grid=(B, ngrp, nq),
        in_specs=[
            pl.BlockSpec((tq, hw), lambda b, h, i: (b * nq + i, qb + h)),
            pl.BlockSpec((S, hw), lambda b, h, i: (b, kb + h)),
            pl.BlockSpec((S, hw), lambda b, h, i: (b, vb + h)),
            pl.BlockSpec((Lc, hw), lambda b, h, i: (b, kb + h)),
            pl.BlockSpec((Lc, hw), lambda b, h, i: (b, vb + h)),
            pl.BlockSpec((4, HEAD_DIM), lambda b, h, i: (0, 0)),
            pl.BlockSpec((1, LANES), lambda b, h, i: (0, 0)),
        ],
        out_specs=pl.BlockSpec((tq, hw), lambda b, h, i: (b * nq + i, h)),
        out_shape=jax.ShapeDtypeStruct((B * S, MIX_W), BF16),
        scratch_shapes=[pltpu.VMEM((nh, S // tk, 2 * tq, tk), F32)],
        compiler_params=_cparams(("parallel", "parallel", "arbitrary")),
        name="diff_attention",
    )(qkv, qkv, qkv, qkv_c, qkv_c, lamp, subln_g)


def _stack_group(q4):
    return jnp.concatenate([_stack_halves(q4[:, :LANES]), _stack_halves(q4[:, LANES:])], axis=0)


def _merge_group(o, t):
    return jnp.concatenate([_merge_halves(o[:2 * t], t), _merge_halves(o[2 * t:], t)], axis=-1)


def _sink_column(sink_ref, c, t):
    group = sink_ref.shape[0] // SWA_KV_HEADS
    return jnp.concatenate([jnp.full((t, 1), sink_ref[c * group + g] * LOG2E, F32) for g in range(group)], axis=0)


def _swa_kernel(sink_ref, q_ref, k_ref, v_ref, kc_ref, vc_ref, mask_ref, o_ref, *, qb):
    c = pl.program_id(1)
    n0 = pl.program_id(2) * qb
    W = SWA_WINDOW
    S = k_ref.shape[0]
    nb = S // W
    kc = kc_ref[...]
    vc = vc_ref[...]
    sink_col = _sink_column(sink_ref, c, W)
    group = sink_ref.shape[0] // SWA_KV_HEADS
    scores = []
    for t in range(qb):
        n = n0 + t
        ks = pl.multiple_of(jnp.clip((n - 1) * W, 0, S - 3 * W), W)
        krows = pl.ds(ks, 3 * W)
        variant = jnp.where(n == 0, 0, jnp.where(n == nb - 1, 2, 1))
        mask = mask_ref[variant]
        qs = _stack_group(q_ref[t * W:(t + 1) * W, :])
        s = _dot_nt(qs, k_ref[krows, :]) + jnp.concatenate([mask] * group, axis=0)
        sc = _dot_nt(qs, kc)
        scores.append((s, sc, krows))
    for t, (s, sc, krows) in enumerate(scores):
        o = _softmax_pv([(s, v_ref[krows, :]), (sc, vc)], extra_logit=sink_col)
        o_ref[t * W:(t + 1) * W, :] = _merge_group(o, W).astype(o_ref.dtype)


def _swa_mask_table():
    W = SWA_WINDOW
    i = np.arange(W)[:, None]
    j = np.arange(3 * W)[None, :]
    tabs = [np.where(np.abs(j - shift - i) <= W, 0.0, MASK_VALUE) for shift in (0, W, 2 * W)]
    return jnp.asarray(np.stack(tabs), F32)


def _swa_attention(qkv, qkv_c, sink, mask, B, S, Lc):
    W = SWA_WINDOW
    nb = S // W
    qb = 16 if nb % 16 == 0 else (8 if nb % 8 == 0 else 1)
    nqb = nb // qb
    return pl.pallas_call(
        functools.partial(_swa_kernel, qb=qb),
        grid=(B, SWA_KV_HEADS, nqb),
        in_specs=[
            pl.BlockSpec(memory_space=pltpu.SMEM),
            pl.BlockSpec((qb * W, 2 * LANES), lambda b, c, n: (b * nqb + n, _QC // 2 + c)),
            pl.BlockSpec((S, LANES), lambda b, c, n: (b, _KC + c)),
            pl.BlockSpec((S, LANES), lambda b, c, n: (b, _VC + c)),
            pl.BlockSpec((Lc, LANES), lambda b, c, n: (b, _KC + c)),
            pl.BlockSpec((Lc, LANES), lambda b, c, n: (b, _VC + c)),
            pl.BlockSpec(mask.shape, lambda b, c, n: (0, 0, 0)),
        ],
        out_specs=pl.BlockSpec((qb * W, 2 * LANES), lambda b, c, n: (b * nqb + n, c)),
        out_shape=jax.ShapeDtypeStruct((B * S, MIX_W), BF16),
        compiler_params=_cparams(("parallel", "parallel", "arbitrary")),
        name="swa_attention",
    )(sink, qkv, qkv, qkv, qkv_c, qkv_c, mask)


def _ctx_attn_kernel(sink_ref, qkv_ref, lamp_ref, g_ref, oa_ref, ob_ref, os_ref, *, lam_init):
    Lc = qkv_ref.shape[0]

    def cols(blk, width=LANES):
        return qkv_ref[:, blk * LANES: blk * LANES + width]

    for p in range(MIX_W // LANES):
        qs = _stack_halves(cols(_QA + p))
        o = _softmax_pv([(_dot_nt(qs, cols(_KA + p)), cols(_VA + p))])
        oa_ref[:, p * LANES:(p + 1) * LANES] = _merge_halves(o, Lc).astype(oa_ref.dtype)

    lam = _lambda_full(lamp_ref, lam_init)
    for h in range(MIX_W // LANES):
        qs = _stack_halves(cols(_QB + h))
        on = _softmax_pv([(_dot_nt(qs, cols(_KB + h)), cols(_VB + h))])
        o = on[:Lc] - lam * on[Lc:]
        ob_ref[:, h * LANES:(h + 1) * LANES] = _subln(o, g_ref[...], lam_init).astype(ob_ref.dtype)

    for c in range(SWA_KV_HEADS):
        qs = _stack_group(cols(_QC + 2 * c, 2 * LANES))
        o = _softmax_pv([(_dot_nt(qs, cols(_KC + c)), cols(_VC + c))],
                        extra_logit=_sink_column(sink_ref, c, Lc))
        os_ref[:, c * 2 * LANES:(c + 1) * 2 * LANES] = _merge_group(o, Lc).astype(os_ref.dtype)


def _ctx_attention(qkv_c, sink, lamp, subln_g, lam_init, B, Lc):
    NQ = qkv_c.shape[1]
    out = jax.ShapeDtypeStruct((B * Lc, MIX_W), BF16)
    ospec = pl.BlockSpec((Lc, MIX_W), lambda b: (b, 0))
    return pl.pallas_call(
        functools.partial(_ctx_attn_kernel, lam_init=lam_init),
        grid=(B,),
        in_specs=[
            pl.BlockSpec(memory_space=pltpu.SMEM),
            pl.BlockSpec((Lc, NQ), lambda b: (b, 0)),
            pl.BlockSpec((4, HEAD_DIM), lambda b: (0, 0)),
            pl.BlockSpec((1, LANES), lambda b: (0, 0)),
        ],
        out_specs=[ospec, ospec, ospec],
        out_shape=[out, out, out],
        compiler_params=_cparams(("parallel",)),
        name="ctx_attention",
    )(sink, qkv_c, lamp, subln_g)


def _shift_down(x, k, t_idx):
    return jnp.where(t_idx >= k, pltpu.roll(x, k, 0), 0.0)


def _shift_up(x, k, t_idx):
    T = x.shape[0]
    return jnp.where(t_idx < T - k, pltpu.roll(x, T - k, 0), 0.0)


def _pool_kernel(u_ref, w_ref, scale_ref, o_ref):
    T = u_ref.shape[0]
    pg = u_ref.shape[1] // len(POOL_WINDOWS)
    t_idx = lax.broadcasted_iota(jnp.int32, (T, pg), 0)
    for g, w in enumerate(POOL_WINDOWS):
        u = u_ref[:, g * pg:(g + 1) * pg]
        back = u
        fwd = u
        span = 1
        while span < w // 2:
            back = back + _shift_down(back, span, t_idx)
            fwd = fwd + _shift_up(fwd, span, t_idx)
            span *= 2
        win = _shift_down(back, 1, t_idx) + fwd
        lo = jnp.clip(t_idx - w // 2, 0, T)
        hi = jnp.clip(t_idx - w // 2 + w, 0, T)
        d = (win / (hi - lo).astype(F32) - u).astype(BF16)
        mixed = _dot(d, w_ref[g])
        o_ref[:, g * pg:(g + 1) * pg] = (mixed * scale_ref[:, g * pg:(g + 1) * pg]).astype(o_ref.dtype)


def _pool_mixer(du, w_grp, scale, nseq, T):
    return pl.pallas_call(
        _pool_kernel,
        grid=(nseq,),
        in_specs=[
            pl.BlockSpec((T, MIX_W), lambda b: (b, 0)),
            pl.BlockSpec(w_grp.shape, lambda b: (0, 0, 0)),
            pl.BlockSpec((1, MIX_W), lambda b: (0, 0)),
        ],
        out_specs=pl.BlockSpec((T, MIX_W), lambda b: (b, 0)),
        out_shape=jax.ShapeDtypeStruct((nseq * T, MIX_W), BF16),
        compiler_params=_cparams(("parallel",)),
        name="pool_mixer",
    )(du, w_grp, scale)


def _merge_kernel(h_ref, oa_ref, ob_ref, os_ref, od_ref, wg0_ref, wg1_ref, wg2_ref, wg3_ref, wb_ref,
                  acc_ref, f32_ref):
    h = h_ref[...]
    branches = zip((oa_ref, ob_ref, os_ref, od_ref), (wg0_ref, wg1_ref, wg2_ref, wg3_ref))
    for n, (o_ref, wg_ref) in enumerate(branches):
        gate = jax.nn.sigmoid(_dot(h, wg_ref[...]))
        term = gate * _dot(o_ref[...], wb_ref[n])
        if n == 0:
            f32_ref[...] = term
        else:
            f32_ref[...] += term
    acc_ref[...] = f32_ref[...].astype(acc_ref.dtype)


def _merge(h, outs, l, w_gate, w_br, tm, tn):
    M, D = h.shape
    nj = D // tn
    ospec = pl.BlockSpec((tm, MIX_W), lambda i, j: (i, 0))
    gate_spec = lambda n: pl.BlockSpec((None, D, tn), lambda i, j: (l, 0, n * nj + j))
    return pl.pallas_call(
        _merge_kernel,
        grid=(M // tm, nj),
        in_specs=[
            pl.BlockSpec((tm, D), lambda i, j: (i, 0)),
            ospec, ospec, ospec, ospec,
            gate_spec(0), gate_spec(1), gate_spec(2), gate_spec(3),
            pl.BlockSpec((None, N_BRANCH, MIX_W, tn), lambda i, j: (l, 0, 0, j)),
        ],
        out_specs=pl.BlockSpec((tm, tn), lambda i, j: (i, j)),
        out_shape=jax.ShapeDtypeStruct((M, D), BF16),
        scratch_shapes=[pltpu.VMEM((tm, tn), F32)],
        compiler_params=_cparams(("parallel", "arbitrary")),
        name="gated_merge",
    )(h, *outs, w_gate, w_gate, w_gate, w_gate, w_br)


def _outproj_kernel(a_ref, w_ref, x_ref, g_ref, o_ref, *, tc):
    a = a_ref[...]
    for c in range(o_ref.shape[1] // tc):
        cs = slice(c * tc, (c + 1) * tc)
        o_ref[:, cs] = x_ref[:, cs] + g_ref[:, cs] * _dot(a, w_ref[:, cs])


def _out_project(acc, l, w_o, x2, mod5, row_of_tile, tm):
    M, D = x2.shape
    return pl.pallas_call(
        functools.partial(_outproj_kernel, tc=512),
        grid=(M // tm,),
        in_specs=[
            pl.BlockSpec((tm, D), lambda i: (i, 0)),
            pl.BlockSpec((None, D, D), lambda i: (l, 0, 0), pipeline_mode=pl.Buffered(1)),
            pl.BlockSpec((tm, D), lambda i: (i, 0)),
            pl.BlockSpec((None, None, None, 1, D), lambda i: (l, row_of_tile(i), 2, 0, 0)),
        ],
        out_specs=pl.BlockSpec((tm, D), lambda i: (i, 0)),
        out_shape=jax.ShapeDtypeStruct((M, D), F32),
        compiler_params=_cparams(("parallel",)),
        name="out_proj",
    )(acc, w_o, x2, mod5)


def _ffn_kernel(x_ref, g_ref, sh_ref, sc_ref, gate_ref, wg_ref, wu_ref, wd_ref, o_ref, h_ref, *, tc, nf):
    f = pl.program_id(1)
    D = o_ref.shape[1]

    last = nf - 1

    @pl.when(f == 0)
    def _():
        _modnorm_into(h_ref, x_ref, g_ref, sh_ref, sc_ref)

    def step(first, final):
        h = h_ref[...]
        a = _dot(h, wg_ref[...])
        u = _dot(h, wu_ref[...])
        act = (a * jax.nn.sigmoid(a) * u).astype(BF16)
        for c in range(D // tc):
            cs = slice(c * tc, (c + 1) * tc)
            part = _dot(act, wd_ref[:, cs])
            total = part if first else o_ref[:, cs] + part
            o_ref[:, cs] = x_ref[:, cs] + gate_ref[:, cs] * total if final else total

    if nf == 1:
        step(True, True)
        return

    @pl.when(f == 0)
    def _():
        step(True, False)

    @pl.when((f > 0) & (f < last))
    def _():
        step(False, False)

    @pl.when(f == last)
    def _():
        step(False, True)


def _ffn(x2, mod5, l, row_of_tile, g_norm, wg, wu, wd, tm, tf):
    M, D = x2.shape
    F = wg.shape[2]
    mod_spec = lambda k: pl.BlockSpec((None, None, None, 1, D), lambda i, f: (l, row_of_tile(i), k, 0, 0))
    return pl.pallas_call(
        functools.partial(_ffn_kernel, tc=512, nf=F // tf),
        grid=(M // tm, F // tf),
        in_specs=[
            pl.BlockSpec((tm, D), lambda i, f: (i, 0)),
            pl.BlockSpec((1, D), lambda i, f: (0, 0)),
            mod_spec(3), mod_spec(4), mod_spec(5),
            pl.BlockSpec((None, D, tf), lambda i, f: (l, 0, f)),
            pl.BlockSpec((None, D, tf), lambda i, f: (l, 0, f)),
            pl.BlockSpec((None, tf, D), lambda i, f: (l, f, 0)),
        ],
        out_specs=pl.BlockSpec((tm, D), lambda i, f: (i, 0)),
        out_shape=jax.ShapeDtypeStruct((M, D), F32),
        scratch_shapes=[pltpu.VMEM((tm, D), BF16)],
        compiler_params=_cparams(("parallel", "arbitrary")),
        name="swiglu_ffn",
    )(x2, g_norm, mod5, mod5, mod5, wg, wu, wd)


def _rope_tables(S):
    t = jnp.arange(S)
    pos = jnp.stack([t // GRID_W, t % GRID_W], axis=-1).astype(F32)
    n_freq = HEAD_DIM // 4
    inv = ROPE_BASE ** (-jnp.arange(n_freq, dtype=F32) / n_freq)
    ang = pos[:, :, None] * inv
    cos, sin = jnp.cos(ang), jnp.sin(ang)
    cos_h = jnp.concatenate([cos, cos], axis=-1).reshape(S, HEAD_DIM)
    sin_h = jnp.concatenate([-sin, sin], axis=-1).reshape(S, HEAD_DIM)
    reps = MIX_W // HEAD_DIM
    return jnp.tile(cos_h, (1, reps)), jnp.tile(sin_h, (1, reps))


def _split_w_in(w_in):
    sizes = (MIX_W,) * 7 + (SWA_KV_HEADS * HEAD_DIM,) * 2 + (MIX_W,)
    offs = [0]
    for s in sizes:
        offs.append(offs[-1] + s)
    parts = [w_in[..., offs[i]:offs[i + 1]] for i in range(len(sizes))]
    return parts, w_in[..., offs[-1]:]


def _dup_heads(w):
    lead = w.shape[:-1]
    w4 = jnp.repeat(w.reshape(lead + (SWA_KV_HEADS, 1, HEAD_DIM)), 2, axis=-2)
    return w4.reshape(lead + (2 * SWA_KV_HEADS * HEAD_DIM,))


def _tile_heads(g, scale=1.0):
    return jnp.tile(g.astype(F32) * scale, MIX_W // HEAD_DIM)


def _proj_column_vectors(gains):
    rows = []
    for g in gains:
        gain = jnp.ones((MIX_W,), F32) if g is None else g
        rows.append(jnp.stack([gain] + [jnp.zeros((MIX_W,), F32)] * 7))
    return jnp.stack(rows)


def _pick_tile(M, unit, target):
    t = min(target, M)
    while M % t or (t > unit and t % unit) or (t < unit and unit % t):
        t //= 2
    return t


def kernel(x, c, ctx, c_ctx, w_ada, b_ada, norm_mix, norm_ffn, w_in, a_q_norm, a_k_norm, a_rpb, b_q_norm, b_k_norm, b_lam_q1, b_lam_k1, b_lam_q2, b_lam_k2, b_subln, c_q_norm, c_k_norm, c_sink, d_w, d_scale, w_branch, w_out, w_ffn_gate, w_ffn_up, w_ffn_down):
    B, S, D = x.shape
    Lc = ctx.shape[1]
    L = w_ada.shape[0]
    assert D == N_BRANCH * MIX_W and S % GRID_W == 0 and S // GRID_W >= NA_ROWS and S >= 3 * SWA_WINDOW

    tm = _pick_tile(S, S, 512)
    tm_proj = _pick_tile(S, S, 1024)
    tm_ffn = _pick_tile(S, S, 1024)
    lat_row_ffn = lambda i: i // (S // tm_ffn)
    tm_merge = _pick_tile(B * S, 8, 1024)
    tmc = _pick_tile(B * Lc, 8, 512)
    lat_row = lambda i: i // (S // tm)
    lat_row_proj = lambda i: i // (S // tm_proj)
    ctx_row = lambda i: B
    n_mod_rows = -(-(B + 1) // 8) * 8

    cc = jnp.zeros((n_mod_rows, D), F32).at[:B].set(c).at[B].set(c_ctx)
    mod = _modulation(cc, w_ada, b_ada).reshape(L, n_mod_rows, 6, 1, D)

    cos_t, sin_t = _rope_tables(S)
    cos_id = jnp.ones((tmc, MIX_W), F32)
    sin_id = jnp.zeros((tmc, MIX_W), F32)
    lane = jnp.arange(2 * LANES)
    bd = (lane[:, None] // HEAD_DIM == lane[None, :] // HEAD_DIM).astype(BF16)
    swa_mask = _swa_mask_table()

    x2 = x.reshape(B * S, D)
    c2 = ctx.reshape(B * Lc, D)
    tn_merge = 512

    (aq, ak, av, bq, bk, bv, sq, sk, sv, du), w_gl = _split_w_in(w_in)
    w_cat = jnp.concatenate([aq, ak, av, bq, bk, bv, sq, _dup_heads(sk), _dup_heads(sv), du], axis=-1).astype(BF16)
    w_gate_r = w_gl.astype(BF16)
    w_br = w_branch.astype(BF16)
    w_o = w_out.astype(BF16)
    wg, wu, wd = w_ffn_gate.astype(BF16), w_ffn_up.astype(BF16), w_ffn_down.astype(BF16)

    for l in range(L):
        last = l == L - 1
        lam_init = 0.8 - 0.6 * math.exp(-0.3 * l)

        pvecs = _proj_column_vectors([
            _tile_heads(a_q_norm[l], Q_SCALE), _tile_heads(a_k_norm[l]), None,
            _tile_heads(b_q_norm[l], Q_SCALE), _tile_heads(b_k_norm[l]), None,
            _tile_heads(c_q_norm[l], Q_SCALE),
            _tile_heads(c_k_norm[l]),
        ])
        g_mix_norm = norm_mix[l].reshape(1, D)
        g_ffn_norm = norm_ffn[l].reshape(1, D)
        lamp = jnp.stack([b_lam_q1[l], b_lam_k1[l], b_lam_q2[l], b_lam_k2[l]]).astype(F32)
        subln_g = b_subln[l].reshape(1, LANES).astype(F32)
        sink = c_sink[l].astype(F32)
        w_pool = d_w[l].astype(BF16)
        pool_scale = d_scale[l].reshape(1, MIX_W).astype(F32)

        qkv, du_l, h_l = _project(x2, mod, l, lat_row_proj, g_mix_norm, w_cat, pvecs, cos_t, sin_t,
                                  lambda i: i % (S // tm_proj), bd, tm_proj)
        qkv_c, du_c, h_c = _project(c2, mod, l, ctx_row, g_mix_norm, w_cat, pvecs, cos_id, sin_id,
                                    lambda i: 0, bd, tmc)

        o_a = _na_attention(qkv, qkv_c, _na_bias_table(a_rpb[l], S // GRID_W), B, S, Lc)
        o_b = _diff_attention(qkv, qkv_c, lamp, subln_g, lam_init, B, S, Lc)
        o_s = _swa_attention(qkv, qkv_c, sink, swa_mask, B, S, Lc)
        o_d = _pool_mixer(du_l, w_pool, pool_scale, B, S)
        acc = _merge(h_l, (o_a, o_b, o_s, o_d), l, w_gate_r, w_br, tm_merge, tn_merge)
        x2 = _out_project(acc, l, w_o, x2, mod, lat_row, tm)

        if not last:
            oc_a, oc_b, oc_s = _ctx_attention(qkv_c, sink, lamp, subln_g, lam_init, B, Lc)
            oc_d = _pool_mixer(du_c, w_pool, pool_scale, B, Lc)
            acc_c = _merge(h_c, (oc_a, oc_b, oc_s, oc_d), l, w_gate_r, w_br, tmc, tn_merge)
            c2 = _out_project(acc_c, l, w_o, c2, mod, ctx_row, tmc)
            c2 = _ffn(c2, mod, l, ctx_row, g_ffn_norm, wg, wu, wd, tmc, 512)

        x2 = _ffn(x2, mod, l, lat_row_ffn, g_ffn_norm, wg, wu, wd, tm_ffn, 512)

    return x2.reshape(B, S, D)
```

```python
import functools
import math

import numpy as np
import jax
import jax.numpy as jnp
from jax import lax
from jax.experimental import pallas as pl
from jax.experimental.pallas import tpu as pltpu

F32 = jnp.float32
BF16 = jnp.bfloat16

HEAD_DIM = 64
LANES = 128
GRID_W = 64
N_BRANCH = 4
NA_ROWS = 8
NA_COLS = 16
SWA_WINDOW = 128
SWA_KV_HEADS = 2
POOL_WINDOWS = (2, 4, 8, 16)
ROPE_BASE = 10000.0
NORM_EPS = 1e-6
MASK_VALUE = -1e30
ATTN_SCALE = HEAD_DIM ** -0.5
LOG2E = math.log2(math.e)
Q_SCALE = ATTN_SCALE * LOG2E
MIX_W = 512
VMEM_LIMIT = 56 * 1024 * 1024

TM_PROJ = 1024
TM_FFN = 1024
TM_MERGE = 1024
TM_OUT = 512
TN_MERGE = 512
TF_FFN = 512


def _cparams(sem):
    return pltpu.CompilerParams(dimension_semantics=sem, vmem_limit_bytes=VMEM_LIMIT)


def _dot(a, b):
    return jnp.dot(a, b, preferred_element_type=F32)


def _dot_nt(a, b):
    return lax.dot_general(a, b, (((1,), (1,)), ((), ())), preferred_element_type=F32)


def _lane_lo(shape):
    return lax.broadcasted_iota(jnp.int32, shape, len(shape) - 1) % LANES < HEAD_DIM


def _stack_halves(q2):
    lo = _lane_lo(q2.shape)
    zero = jnp.zeros_like(q2)
    return jnp.concatenate([jnp.where(lo, q2, zero), jnp.where(lo, zero, q2)], axis=0)


def _merge_halves(o, t):
    lo = _lane_lo((t, LANES))
    return jnp.where(lo, o[:t], o[t:])


def _mod_kernel(c_ref, w_ref, b_ref, o_ref):
    c = c_ref[...]
    act = (c * jax.nn.sigmoid(c)).astype(BF16)
    o_ref[...] = _dot(act, w_ref[...].astype(BF16)) + b_ref[...]


def _modulation(cc, w_ada, b_ada):
    L, D, N = w_ada.shape
    R = cc.shape[0]
    tn = 1024
    return pl.pallas_call(
        _mod_kernel,
        grid=(L, N // tn),
        in_specs=[
            pl.BlockSpec((R, D), lambda l, j: (0, 0)),
            pl.BlockSpec((None, D, tn), lambda l, j: (l, 0, j)),
            pl.BlockSpec((None, 1, tn), lambda l, j: (l, 0, j)),
        ],
        out_specs=pl.BlockSpec((None, R, tn), lambda l, j: (l, 0, j)),
        out_shape=jax.ShapeDtypeStruct((L, R, N), F32),
        compiler_params=_cparams(("parallel", "parallel")),
        name="adaln_mod",
    )(cc, w_ada, b_ada.reshape(L, 1, N))


def _modnorm_into(dst_ref, x_ref, g_ref, sh_ref, sc_ref, chunk=128):
    tm = x_ref.shape[0]
    sh = sh_ref[...]
    gain = g_ref[...] * (1.0 + sc_ref[...])

    for r in range(tm // chunk):
        rows = slice(r * chunk, (r + 1) * chunk)
        xf = x_ref[rows, :]
        ms = jnp.mean(xf * xf, axis=-1, keepdims=True)
        dst_ref[rows, :] = (xf * lax.rsqrt(ms + NORM_EPS) * gain + sh).astype(dst_ref.dtype)


def _proj_kernel(x_ref, g_ref, sh_ref, sc_ref, w_ref, pv_ref, cos_ref, sin_ref, bd_ref,
                 qkv_ref, du_ref, h_ref, y_ref, *, kinds):
    j = pl.program_id(1)

    def headnorm(yp):
        y2 = (yp * yp).astype(BF16)
        hw = bd_ref.shape[0]
        parts = [_dot(y2[:, c * hw:(c + 1) * hw], bd_ref[...]) for c in range(y2.shape[1] // hw)]
        ss = parts[0] if len(parts) == 1 else jnp.concatenate(parts, axis=1)
        return yp * (lax.rsqrt(ss * (1.0 / HEAD_DIM) + NORM_EPS) * pv_ref[0:1, :yp.shape[1]])

    def rope(n):
        w = n.shape[-1]
        lane = lax.broadcasted_iota(jnp.int32, n.shape, 1)
        partner = jnp.where((lane % 32) < 16, pltpu.roll(n, w - 16, 1), pltpu.roll(n, 16, 1))
        return n * cos_ref[:, :w] + partner * sin_ref[:, :w]

    def half_norm_rope(yp):
        half = yp.shape[1] // 2
        return jnp.concatenate([rope(headnorm(yp[:, :half])), yp[:, half:]], axis=1)

    epilogues = {
        "plain": lambda yp: yp,
        "norm": headnorm,
        "norm_rope": lambda yp: rope(headnorm(yp)),
        "half_norm_rope": half_norm_rope,
    }

    def step(epilogue):
        if epilogue is not None:
            qkv_ref[...] = epilogue(y_ref[...]).astype(BF16)
        y = _dot(h_ref[...], w_ref[...])
        y_ref[...] = y

        @pl.when(j == pl.num_programs(1) - 1)
        def _():
            du_ref[...] = y

    @pl.when(j == 0)
    def _():
        _modnorm_into(h_ref, x_ref, g_ref, sh_ref, sc_ref)
        step(None)

    for kind, epilogue in epilogues.items():
        steps = [b + 1 for b, k in enumerate(kinds) if k == kind]
        if steps:
            pl.when(functools.reduce(lambda a, b: a | b, [j == s for s in steps]))(
                functools.partial(step, epilogue))


_QKV_BLOCK_KINDS = ("norm", "norm", "plain", "norm_rope", "norm_rope", "plain", "norm_rope", "half_norm_rope")


def _project(x2, mod5, l, row_of_tile, g_norm, w_cat, pvecs, cos_t, sin_t, table_block, bd, tm):
    M, D = x2.shape
    nblk = w_cat.shape[2] // MIX_W
    nq = nblk - 1
    mod_spec = lambda k: pl.BlockSpec((None, None, None, 1, D), lambda i, j: (l, row_of_tile(i), k, 0, 0))
    assert len(_QKV_BLOCK_KINDS) == nq
    return pl.pallas_call(
        functools.partial(_proj_kernel, kinds=_QKV_BLOCK_KINDS),
        grid=(M // tm, nblk),
        in_specs=[
            pl.BlockSpec((tm, D), lambda i, j: (i, 0)),
            pl.BlockSpec((1, D), lambda i, j: (0, 0)),
            mod_spec(0), mod_spec(1),
            pl.BlockSpec((None, D, MIX_W), lambda i, j: (l, 0, j)),
            pl.BlockSpec((None, 8, MIX_W), lambda i, j: (jnp.maximum(j - 1, 0), 0, 0)),
            pl.BlockSpec((tm, MIX_W), lambda i, j: (table_block(i), 0)),
            pl.BlockSpec((tm, MIX_W), lambda i, j: (table_block(i), 0)),
            pl.BlockSpec(bd.shape, lambda i, j: (0, 0)),
        ],
        out_specs=[
            pl.BlockSpec((tm, MIX_W), lambda i, j: (i, jnp.maximum(j - 1, 0))),
            pl.BlockSpec((tm, MIX_W), lambda i, j: (i, 0)),
            pl.BlockSpec((tm, D), lambda i, j: (i, 0)),
        ],
        out_shape=[
            jax.ShapeDtypeStruct((M, nq * MIX_W), BF16),
            jax.ShapeDtypeStruct((M, MIX_W), F32),
            jax.ShapeDtypeStruct((M, D), BF16),
        ],
        scratch_shapes=[pltpu.VMEM((tm, MIX_W), F32)],
        compiler_params=_cparams(("parallel", "arbitrary")),
        name="in_proj",
    )(x2, g_norm, mod5, mod5, w_cat, pvecs, cos_t, sin_t, bd)


_QA, _KA, _VA = 0, 4, 8
_QB, _KB, _VB = 12, 16, 20
_QC = 24
_KC, _VC = 28, 30


def _softmax_pv(parts, extra_logit=None):
    s = jnp.concatenate([s for s, _ in parts], axis=1) if len(parts) > 1 else parts[0][0]
    m = s.max(axis=-1, keepdims=True)
    if extra_logit is not None:
        m = jnp.maximum(m, extra_logit)
    p = jnp.exp2(s - m)
    l = p.sum(axis=-1, keepdims=True)
    if extra_logit is not None:
        l = l + jnp.exp2(extra_logit - m)
    pb = p.astype(BF16)
    o = None
    start = 0
    for sp, v in parts:
        width = sp.shape[1]
        term = _dot(pb[:, start:start + width], v)
        o = term if o is None else o + term
        start += width
    return o / l


NA_GROUP = 4
NA_UNION = NA_GROUP + NA_ROWS


def _na_kernel(q_ref, k_ref, v_ref, kc_ref, vc_ref, bias_ref, o_ref, *, rows, units):
    g0 = pl.program_id(2) * units
    n_groups = rows // NA_GROUP
    tq = NA_GROUP * GRID_W
    kc = kc_ref[...]
    vc = vc_ref[...]
    scores = []
    for u in range(units):
        g = g0 + u
        start = jnp.clip(NA_GROUP * g - NA_ROWS // 2, 0, rows - NA_UNION)
        variant = jnp.where(g == 0, 0, jnp.where(g == n_groups - 1, 2, 1))
        krows = pl.ds(pl.multiple_of(start * GRID_W, GRID_W), NA_UNION * GRID_W)
        qs = _stack_halves(q_ref[u * tq:(u + 1) * tq, :])
        s = _dot_nt(qs, k_ref[krows, :]) + bias_ref[variant]
        sc = _dot_nt(qs, kc)
        scores.append((s, sc, krows))
    for u, (s, sc, krows) in enumerate(scores):
        o = _softmax_pv([(s, v_ref[krows, :]), (sc, vc)])
        o_ref[u * tq:(u + 1) * tq, :] = _merge_halves(o, tq).astype(o_ref.dtype)


def _na_attention(qkv, qkv_c, bias, B, S, Lc):
    rows = S // GRID_W
    assert rows % NA_GROUP == 0 and rows >= NA_UNION
    n_groups = rows // NA_GROUP
    units = 8 if n_groups % 8 == 0 else (4 if n_groups % 4 == 0 else 1)
    nstep = n_groups // units
    tq = units * NA_GROUP * GRID_W
    npair = MIX_W // LANES
    return pl.pallas_call(
        functools.partial(_na_kernel, rows=rows, units=units),
        grid=(B, npair, nstep),
        in_specs=[
            pl.BlockSpec((tq, LANES), lambda b, p, r: (b * nstep + r, _QA + p)),
            pl.BlockSpec((S, LANES), lambda b, p, r: (b, _KA + p)),
            pl.BlockSpec((S, LANES), lambda b, p, r: (b, _VA + p)),
            pl.BlockSpec((Lc, LANES), lambda b, p, r: (b, _KA + p)),
            pl.BlockSpec((Lc, LANES), lambda b, p, r: (b, _VA + p)),
            pl.BlockSpec((None,) + bias.shape[1:], lambda b, p, r: (p, 0, 0, 0)),
        ],
        out_specs=pl.BlockSpec((tq, LANES), lambda b, p, r: (b * nstep + r, p)),
        out_shape=jax.ShapeDtypeStruct((B * S, MIX_W), BF16),
        compiler_params=_cparams(("parallel", "parallel", "arbitrary")),
        name="na_attention",
    )(qkv, qkv, qkv, qkv_c, qkv_c, bias)


def _na_bias_kernel(rpb_ref, onehot_ref, valid_ref, o_ref):
    r = rpb_ref[...]
    hi = r.astype(BF16)
    r1 = r - hi.astype(F32)
    mid = r1.astype(BF16)
    lo = (r1 - mid.astype(F32)).astype(BF16)
    oh = onehot_ref[...]
    t = _dot(hi, oh) + _dot(mid, oh) + _dot(lo, oh)
    o_ref[...] = jnp.where(valid_ref[...] > 0.0, t * LOG2E, MASK_VALUE)


def _na_window_pattern(rows, g):
    start_u = np.clip(NA_GROUP * g - NA_ROWS // 2, 0, rows - NA_UNION)
    r = NA_GROUP * g + np.arange(NA_GROUP)[:, None]
    start_r = np.clip(r - NA_ROWS // 2, 0, rows - NA_ROWS)
    kr = start_u + np.arange(NA_UNION)[None, :]
    valid = (kr >= start_r) & (kr < start_r + NA_ROWS)
    return np.where(valid, kr - r + (NA_ROWS - 1), 0), valid


def _na_bias_table(rpb, rows):
    H, n_dr, n_dc = rpb.shape
    n_groups = rows // NA_GROUP
    patterns = [_na_window_pattern(rows, g) for g in (0, 1, n_groups - 1)]
    for g in range(1, n_groups - 1):
        dr_g, valid_g = _na_window_pattern(rows, g)
        assert (dr_g == patterns[1][0]).all() and (valid_g == patterns[1][1]).all()
    col = np.arange(GRID_W)
    col_start = np.clip(col - NA_COLS // 2, 0, GRID_W - NA_COLS)
    valid = (col[None, :] >= col_start[:, None]) & (col[None, :] < col_start[:, None] + NA_COLS)
    dc = np.clip(col[None, :] - col[:, None], -(NA_COLS - 1), NA_COLS - 1) + (NA_COLS - 1)
    onehot = (np.arange(LANES)[:, None] == dc.reshape(1, -1)).astype(np.float32)
    n_rows = -(-(H * n_dr) // LANES) * LANES
    rpb_rows = jnp.zeros((n_rows, LANES), F32).at[:H * n_dr, :n_dc].set(rpb.reshape(H * n_dr, n_dc).astype(F32))
    t = pl.pallas_call(
        _na_bias_kernel,
        out_shape=jax.ShapeDtypeStruct((n_rows, GRID_W * GRID_W), F32),
        name="na_bias",
    )(rpb_rows, jnp.asarray(onehot, BF16), jnp.asarray(valid.reshape(1, -1), F32))
    t3 = t.reshape(n_rows, GRID_W, GRID_W)
    out_rows, out_cols = 2 * NA_GROUP * GRID_W, NA_UNION * GRID_W
    return pl.pallas_call(
        functools.partial(_na_assemble_kernel, patterns=patterns, n_dr=n_dr),
        grid=(H // 2,),
        in_specs=[pl.BlockSpec(t3.shape, lambda p: (0, 0, 0))],
        out_specs=pl.BlockSpec((None, len(patterns), out_rows, out_cols), lambda p: (p, 0, 0, 0)),
        out_shape=jax.ShapeDtypeStruct((H // 2, len(patterns), out_rows, out_cols), F32),
        compiler_params=_cparams(("parallel",)),
        name="na_bias_assemble",
    )(t3)


def _na_assemble_kernel(t3_ref, o_ref, *, patterns, n_dr):
    p = pl.program_id(0)
    masked = jnp.full((GRID_W, GRID_W), MASK_VALUE, F32)
    for v, (dr, ok) in enumerate(patterns):
        for parity in range(2):
            base = (2 * p + parity) * n_dr
            for rr in range(NA_GROUP):
                r0 = (parity * NA_GROUP + rr) * GRID_W
                for j in range(NA_UNION):
                    tile = t3_ref[base + int(dr[rr, j])] if ok[rr, j] else masked
                    o_ref[v, r0:r0 + GRID_W, j * GRID_W:(j + 1) * GRID_W] = tile


def _lambda_full(lamp_ref, lam_init):
    lp = lamp_ref[...]
    a = jnp.sum(lp[0:1] * lp[1:2], axis=-1, keepdims=True)
    b = jnp.sum(lp[2:3] * lp[3:4], axis=-1, keepdims=True)
    return jnp.exp(a) - jnp.exp(b) + lam_init


def _subln(o, g, lam_init):
    ms = jnp.mean(o * o, axis=-1, keepdims=True)
    return o * lax.rsqrt(ms + NORM_EPS) * g * (1.0 - lam_init)


def _fold_lanes(x, op):
    out = x[:, :LANES]
    for b in range(1, x.shape[1] // LANES):
        out = op(out, x[:, b * LANES:(b + 1) * LANES])
    return out


def _diff_kernel(q_ref, k_ref, v_ref, kc_ref, vc_ref, lamp_ref, g_ref, o_ref, s_ref, *, lam_init, tk):
    tq = q_ref.shape[0]
    S = k_ref.shape[0]
    nkb = S // tk
    lam = _lambda_full(lamp_ref, lam_init)
    nh = q_ref.shape[1] // LANES
    heads = [slice(h * LANES, (h + 1) * LANES) for h in range(nh)]
    qs = [_stack_halves(q_ref[:, hs]) for hs in heads]
    lane_max, m, lane_sum, acc = [None] * nh, [None] * nh, [None] * nh, [None] * nh

    def pass1_block(h, kb):
        s = _dot_nt(qs[h], k_ref[kb * tk:(kb + 1) * tk, heads[h]])
        s_ref[h, kb] = s
        blk = _fold_lanes(s, jnp.maximum)
        lane_max[h] = blk if lane_max[h] is None else jnp.maximum(lane_max[h], blk)

    def pass1_finish(h):
        sc = _dot_nt(qs[h], kc_ref[:, heads[h]])
        m[h] = jnp.maximum(lane_max[h], _fold_lanes(sc, jnp.maximum)).max(axis=-1, keepdims=True)
        p = jnp.exp2(sc - m[h])
        lane_sum[h] = _fold_lanes(p, jnp.add)
        acc[h] = _dot(p.astype(BF16), vc_ref[:, heads[h]])

    def pass2_block(h, kb):
        p = jnp.exp2(s_ref[h, kb] - m[h])
        lane_sum[h] = lane_sum[h] + _fold_lanes(p, jnp.add)
        acc[h] = acc[h] + _dot(p.astype(BF16), v_ref[kb * tk:(kb + 1) * tk, heads[h]])

    def pass2_finish(h):
        on = acc[h] / lane_sum[h].sum(axis=-1, keepdims=True)
        o = on[:tq] - lam * on[tq:]
        o_ref[:, heads[h]] = _subln(o, g_ref[...], lam_init).astype(o_ref.dtype)

    for t in range(nh + 1):
        for kb in range(nkb):
            if t < nh:
                pass1_block(t, kb)
            if t >= 1:
                pass2_block(t - 1, kb)
        if t < nh:
            pass1_finish(t)
        if t >= 1:
            pass2_finish(t - 1)


def _diff_attention(qkv, qkv_c, lamp, subln_g, lam_init, B, S, Lc):
    tq = 256
    nq = S // tq
    hw = 4 * LANES
    ngrp = MIX_W // hw
    qb, kb, vb = _QB * LANES // hw, _KB * LANES // hw, _VB * LANES // hw
    tk = 512
    nh = hw // LANES
    return pl.pallas_call(
        functools.partial(_diff_kernel, lam_init=lam_init, tk=tk),
        grid=(B, ngrp, nq),
        in_specs=[
            pl.BlockSpec((tq, hw), lambda b, h, i: (b * nq + i, qb + h)),
            pl.BlockSpec((S, hw), lambda b, h, i: (b, kb + h)),
            pl.BlockSpec((S, hw), lambda b, h, i: (b, vb + h)),
            pl.BlockSpec((Lc, hw), lambda b, h, i: (b, kb + h)),
            pl.BlockSpec((Lc, hw), lambda b, h, i: (b, vb + h)),
            pl.BlockSpec((4, HEAD_DIM), lambda b, h, i: (0, 0)),
            pl.BlockSpec((1, LANES), lambda b, h, i: (0, 0)),
        ],
        out_specs=pl.BlockSpec((tq, hw), lambda b, h, i: (b * nq + i, h)),
        out_shape=jax.ShapeDtypeStruct((B * S, MIX_W), BF16),
        scratch_shapes=[pltpu.VMEM((nh, S // tk, 2 * tq, tk), F32)],
        compiler_params=_cparams(("parallel", "parallel", "arbitrary")),
        name="diff_attention",
    )(qkv, qkv, qkv, qkv_c, qkv_c, lamp, subln_g)


def _stack_group(q4):
    return jnp.concatenate([_stack_halves(q4[:, :LANES]), _stack_halves(q4[:, LANES:])], axis=0)


def _merge_group(o, t):
    return jnp.concatenate([_merge_halves(o[:2 * t], t), _merge_halves(o[2 * t:], t)], axis=-1)


def _sink_column(sink_ref, c, t):
    group = sink_ref.shape[0] // SWA_KV_HEADS
    return jnp.concatenate([jnp.full((t, 1), sink_ref[c * group + g] * LOG2E, F32) for g in range(group)], axis=0)


def _swa_kernel(sink_ref, q_ref, k_ref, v_ref, kc_ref, vc_ref, mask_ref, o_ref, *, qb):
    c = pl.program_id(1)
    n0 = pl.program_id(2) * qb
    W = SWA_WINDOW
    S = k_ref.shape[0]
    nb = S // W
    kc = kc_ref[...]
    vc = vc_ref[...]
    sink_col = _sink_column(sink_ref, c, W)
    group = sink_ref.shape[0] // SWA_KV_HEADS
    scores = []
    for t in range(qb):
        n = n0 + t
        ks = pl.multiple_of(jnp.clip((n - 1) * W, 0, S - 3 * W), W)
        krows = pl.ds(ks, 3 * W)
        variant = jnp.where(n == 0, 0, jnp.where(n == nb - 1, 2, 1))
        mask = mask_ref[variant]
        qs = _stack_group(q_ref[t * W:(t + 1) * W, :])
        s = _dot_nt(qs, k_ref[krows, :]) + jnp.concatenate([mask] * group, axis=0)
        sc = _dot_nt(qs, kc)
        scores.append((s, sc, krows))
    for t, (s, sc, krows) in enumerate(scores):
        o = _softmax_pv([(s, v_ref[krows, :]), (sc, vc)], extra_logit=sink_col)
        o_ref[t * W:(t + 1) * W, :] = _merge_group(o, W).astype(o_ref.dtype)


def _swa_mask_table():
    W = SWA_WINDOW
    i = np.arange(W)[:, None]
    j = np.arange(3 * W)[None, :]
    tabs = [np.where(np.abs(j - shift - i) <= W, 0.0, MASK_VALUE) for shift in (0, W, 2 * W)]
    return jnp.asarray(np.stack(tabs), F32)


def _swa_attention(qkv, qkv_c, sink, mask, B, S, Lc):
    W = SWA_WINDOW
    nb = S // W
    qb = 16 if nb % 16 == 0 else (8 if nb % 8 == 0 else 1)
    nqb = nb // qb
    return pl.pallas_call(
        functools.partial(_swa_kernel, qb=qb),
        grid=(B, SWA_KV_HEADS, nqb),
        in_specs=[
            pl.BlockSpec(memory_space=pltpu.SMEM),
            pl.BlockSpec((qb * W, 2 * LANES), lambda b, c, n: (b * nqb + n, _QC // 2 + c)),
            pl.BlockSpec((S, LANES), lambda b, c, n: (b, _KC + c)),
            pl.BlockSpec((S, LANES), lambda b, c, n: (b, _VC + c)),
            pl.BlockSpec((Lc, LANES), lambda b, c, n: (b, _KC + c)),
            pl.BlockSpec((Lc, LANES), lambda b, c, n: (b, _VC + c)),
            pl.BlockSpec(mask.shape, lambda b, c, n: (0, 0, 0)),
        ],
        out_specs=pl.BlockSpec((qb * W, 2 * LANES), lambda b, c, n: (b * nqb + n, c)),
        out_shape=jax.ShapeDtypeStruct((B * S, MIX_W), BF16),
        compiler_params=_cparams(("parallel", "parallel", "arbitrary")),
        name="swa_attention",
    )(sink, qkv, qkv, qkv, qkv_c, qkv_c, mask)


def _ctx_attn_kernel(sink_ref, qkv_ref, lamp_ref, g_ref, oa_ref, ob_ref, os_ref, *, lam_init):
    Lc = qkv_ref.shape[0]

    def cols(blk, width=LANES):
        return qkv_ref[:, blk * LANES: blk * LANES + width]

    for p in range(MIX_W // LANES):
        qs = _stack_halves(cols(_QA + p))
        o = _softmax_pv([(_dot_nt(qs, cols(_KA + p)), cols(_VA + p))])
        oa_ref[:, p * LANES:(p + 1) * LANES] = _merge_halves(o, Lc).astype(oa_ref.dtype)

    lam = _lambda_full(lamp_ref, lam_init)
    for h in range(MIX_W // LANES):
        qs = _stack_halves(cols(_QB + h))
        on = _softmax_pv([(_dot_nt(qs, cols(_KB + h)), cols(_VB + h))])
        o = on[:Lc] - lam * on[Lc:]
        ob_ref[:, h * LANES:(h + 1) * LANES] = _subln(o, g_ref[...], lam_init).astype(ob_ref.dtype)

    for c in range(SWA_KV_HEADS):
        qs = _stack_group(cols(_QC + 2 * c, 2 * LANES))
        o = _softmax_pv([(_dot_nt(qs, cols(_KC + c)), cols(_VC + c))],
                        extra_logit=_sink_column(sink_ref, c, Lc))
        os_ref[:, c * 2 * LANES:(c + 1) * 2 * LANES] = _merge_group(o, Lc).astype(os_ref.dtype)


def _ctx_attention(qkv_c, sink, lamp, subln_g, lam_init, B, Lc):
    NQ = qkv_c.shape[1]
    out = jax.ShapeDtypeStruct((B * Lc, MIX_W), BF16)
    ospec = pl.BlockSpec((Lc, MIX_W), lambda b: (b, 0))
    return pl.pallas_call(
        functools.partial(_ctx_attn_kernel, lam_init=lam_init),
        grid=(B,),
        in_specs=[
            pl.BlockSpec(memory_space=pltpu.SMEM),
            pl.BlockSpec((Lc, NQ), lambda b: (b, 0)),
            pl.BlockSpec((4, HEAD_DIM), lambda b: (0, 0)),
            pl.BlockSpec((1, LANES), lambda b: (0, 0)),
        ],
        out_specs=[ospec, ospec, ospec],
        out_shape=[out, out, out],
        compiler_params=_cparams(("parallel",)),
        name="ctx_attention",
    )(sink, qkv_c, lamp, subln_g)


def _shift_down(x, k, t_idx):
    return jnp.where(t_idx >= k, pltpu.roll(x, k, 0), 0.0)


def _shift_up(x, k, t_idx):
    T = x.shape[0]
    return jnp.where(t_idx < T - k, pltpu.roll(x, T - k, 0), 0.0)


def _pool_kernel(u_ref, w_ref, scale_ref, o_ref):
    T = u_ref.shape[0]
    pg = u_ref.shape[1] // len(POOL_WINDOWS)
    t_idx = lax.broadcasted_iota(jnp.int32, (T, pg), 0)
    for g, w in enumerate(POOL_WINDOWS):
        u = u_ref[:, g * pg:(g + 1) * pg]
        back = u
        fwd = u
        span = 1
        while span < w // 2:
            back = back + _shift_down(back, span, t_idx)
            fwd = fwd + _shift_up(fwd, span, t_idx)
            span *= 2
        win = _shift_down(back, 1, t_idx) + fwd
        lo = jnp.clip(t_idx - w // 2, 0, T)
        hi = jnp.clip(t_idx - w // 2 + w, 0, T)
        d = (win / (hi - lo).astype(F32) - u).astype(BF16)
        mixed = _dot(d, w_ref[g])
        o_ref[:, g * pg:(g + 1) * pg] = (mixed * scale_ref[:, g * pg:(g + 1) * pg]).astype(o_ref.dtype)


def _pool_mixer(du, w_grp, scale, nseq, T):
    return pl.pallas_call(
        _pool_kernel,
        grid=(nseq,),
        in_specs=[
            pl.BlockSpec((T, MIX_W), lambda b: (b, 0)),
            pl.BlockSpec(w_grp.shape, lambda b: (0, 0, 0)),
            pl.BlockSpec((1, MIX_W), lambda b: (0, 0)),
        ],
        out_specs=pl.BlockSpec((T, MIX_W), lambda b: (b, 0)),
        out_shape=jax.ShapeDtypeStruct((nseq * T, MIX_W), BF16),
        compiler_params=_cparams(("parallel",)),
        name="pool_mixer",
    )(du, w_grp, scale)


def _merge_kernel(h_ref, oa_ref, ob_ref, os_ref, od_ref, wg0_ref, wg1_ref, wg2_ref, wg3_ref, wb_ref,
                  acc_ref, f32_ref):
    h = h_ref[...]
    branches = zip((oa_ref, ob_ref, os_ref, od_ref), (wg0_ref, wg1_ref, wg2_ref, wg3_ref))
    for n, (o_ref, wg_ref) in enumerate(branches):
        gate = jax.nn.sigmoid(_dot(h, wg_ref[...]))
        term = gate * _dot(o_ref[...], wb_ref[n])
        if n == 0:
            f32_ref[...] = term
        else:
            f32_ref[...] += term
    acc_ref[...] = f32_ref[...].astype(acc_ref.dtype)


def _merge(h, outs, l, w_gate, w_br, tm, tn):
    M, D = h.shape
    nj = D // tn
    ospec = pl.BlockSpec((tm, MIX_W), lambda i, j: (i, 0))
    gate_spec = lambda n: pl.BlockSpec((None, D, tn), lambda i, j: (l, 0, n * nj + j))
    return pl.pallas_call(
        _merge_kernel,
        grid=(M // tm, nj),
        in_specs=[
            pl.BlockSpec((tm, D), lambda i, j: (i, 0)),
            ospec, ospec, ospec, ospec,
            gate_spec(0), gate_spec(1), gate_spec(2), gate_spec(3),
            pl.BlockSpec((None, N_BRANCH, MIX_W, tn), lambda i, j: (l, 0, 0, j)),
        ],
        out_specs=pl.BlockSpec((tm, tn), lambda i, j: (i, j)),
        out_shape=jax.ShapeDtypeStruct((M, D), BF16),
        scratch_shapes=[pltpu.VMEM((tm, tn), F32)],
        compiler_params=_cparams(("parallel", "arbitrary")),
        name="gated_merge",
    )(h, *outs, w_gate, w_gate, w_gate, w_gate, w_br)


def _outproj_kernel(a_ref, w_ref, x_ref, g_ref, o_ref, *, tc):
    a = a_ref[...]
    for c in range(o_ref.shape[1] // tc):
        cs = slice(c * tc, (c + 1) * tc)
        o_ref[:, cs] = x_ref[:, cs] + g_ref[:, cs] * _dot(a, w_ref[:, cs])


def _out_project(acc, l, w_o, x2, mod5, row_of_tile, tm):
    M, D = x2.shape
    return pl.pallas_call(
        functools.partial(_outproj_kernel, tc=512),
        grid=(M // tm,),
        in_specs=[
            pl.BlockSpec((tm, D), lambda i: (i, 0)),
            pl.BlockSpec((None, D, D), lambda i: (l, 0, 0), pipeline_mode=pl.Buffered(1)),
            pl.BlockSpec((tm, D), lambda i: (i, 0)),
            pl.BlockSpec((None, None, None, 1, D), lambda i: (l, row_of_tile(i), 2, 0, 0)),
        ],
        out_specs=pl.BlockSpec((tm, D), lambda i: (i, 0)),
        out_shape=jax.ShapeDtypeStruct((M, D), F32),
        compiler_params=_cparams(("parallel",)),
        name="out_proj",
    )(acc, w_o, x2, mod5)


def _ffn_kernel(x_ref, g_ref, sh_ref, sc_ref, gate_ref, wg_ref, wu_ref, wd_ref, o_ref, h_ref, *, tc, nf):
    f = pl.program_id(1)
    D = o_ref.shape[1]

    last = nf - 1

    def step(first, final):
        if first:
            _modnorm_into(h_ref, x_ref, g_ref, sh_ref, sc_ref)
        h = h_ref[...]
        a = _dot(h, wg_ref[...])
        u = _dot(h, wu_ref[...])
        act = (a * jax.nn.sigmoid(a) * u).astype(BF16)
        for c in range(D // tc):
            cs = slice(c * tc, (c + 1) * tc)
            part = _dot(act, wd_ref[:, cs])
            total = part if first else o_ref[:, cs] + part
            o_ref[:, cs] = x_ref[:, cs] + gate_ref[:, cs] * total if final else total

    if nf == 1:
        step(True, True)
        return

    @pl.when(f == 0)
    def _():
        step(True, False)

    @pl.when((f > 0) & (f < last))
    def _():
        step(False, False)

    @pl.when(f == last)
    def _():
        step(False, True)


def _ffn(x2, mod5, l, row_of_tile, g_norm, wg, wu, wd, tm, tf):
    M, D = x2.shape
    F = wg.shape[2]
    mod_spec = lambda k: pl.BlockSpec((None, None, None, 1, D), lambda i, f: (l, row_of_tile(i), k, 0, 0))
    return pl.pallas_call(
        functools.partial(_ffn_kernel, tc=512, nf=F // tf),
        grid=(M // tm, F // tf),
        in_specs=[
            pl.BlockSpec((tm, D), lambda i, f: (i, 0)),
            pl.BlockSpec((1, D), lambda i, f: (0, 0)),
            mod_spec(3), mod_spec(4), mod_spec(5),
            pl.BlockSpec((None, D, tf), lambda i, f: (l, 0, f)),
            pl.BlockSpec((None, D, tf), lambda i, f: (l, 0, f)),
            pl.BlockSpec((None, tf, D), lambda i, f: (l, f, 0)),
        ],
        out_specs=pl.BlockSpec((tm, D), lambda i, f: (i, 0)),
        out_shape=jax.ShapeDtypeStruct((M, D), F32),
        scratch_shapes=[pltpu.VMEM((tm, D), BF16)],
        compiler_params=_cparams(("parallel", "arbitrary")),
        name="swiglu_ffn",
    )(x2, g_norm, mod5, mod5, mod5, wg, wu, wd)


def _rope_tables(S):
    t = jnp.arange(S)
    pos = jnp.stack([t // GRID_W, t % GRID_W], axis=-1).astype(F32)
    n_freq = HEAD_DIM // 4
    inv = ROPE_BASE ** (-jnp.arange(n_freq, dtype=F32) / n_freq)
    ang = pos[:, :, None] * inv
    cos, sin = jnp.cos(ang), jnp.sin(ang)
    cos_h = jnp.concatenate([cos, cos], axis=-1).reshape(S, HEAD_DIM)
    sin_h = jnp.concatenate([-sin, sin], axis=-1).reshape(S, HEAD_DIM)
    reps = MIX_W // HEAD_DIM
    return jnp.tile(cos_h, (1, reps)), jnp.tile(sin_h, (1, reps))


def _split_w_in(w_in):
    sizes = (MIX_W,) * 7 + (SWA_KV_HEADS * HEAD_DIM,) * 2 + (MIX_W,)
    offs = [0]
    for s in sizes:
        offs.append(offs[-1] + s)
    parts = [w_in[..., offs[i]:offs[i + 1]] for i in range(len(sizes))]
    return parts, w_in[..., offs[-1]:]


def _dup_heads(w):
    lead = w.shape[:-1]
    w4 = jnp.repeat(w.reshape(lead + (SWA_KV_HEADS, 1, HEAD_DIM)), 2, axis=-2)
    return w4.reshape(lead + (2 * SWA_KV_HEADS * HEAD_DIM,))


def _tile_heads(g, scale=1.0):
    return jnp.tile(g.astype(F32) * scale, MIX_W // HEAD_DIM)


def _proj_column_vectors(gains):
    rows = []
    for g in gains:
        gain = jnp.ones((MIX_W,), F32) if g is None else g
        rows.append(jnp.stack([gain] + [jnp.zeros((MIX_W,), F32)] * 7))
    return jnp.stack(rows)


def _pick_tile(M, unit, target):
    t = min(target, M)
    while M % t or (t > unit and t % unit) or (t < unit and unit % t):
        t //= 2
    return t


def kernel(x, c, ctx, c_ctx, w_ada, b_ada, norm_mix, norm_ffn, w_in, a_q_norm, a_k_norm, a_rpb, b_q_norm, b_k_norm, b_lam_q1, b_lam_k1, b_lam_q2, b_lam_k2, b_subln, c_q_norm, c_k_norm, c_sink, d_w, d_scale, w_branch, w_out, w_ffn_gate, w_ffn_up, w_ffn_down):
    B, S, D = x.shape
    Lc = ctx.shape[1]
    L = w_ada.shape[0]
    assert D == N_BRANCH * MIX_W and S % GRID_W == 0 and S // GRID_W >= NA_ROWS and S >= 3 * SWA_WINDOW

    tm = _pick_tile(S, S, TM_OUT)
    tm_proj = _pick_tile(S, S, TM_PROJ)
    tm_ffn = _pick_tile(S, S, TM_FFN)
    tm_merge = _pick_tile(B * S, 8, TM_MERGE)
    tmc = _pick_tile(B * Lc, 8, TM_OUT)
    tmc_proj = _pick_tile(B * Lc, 8, TM_PROJ)
    tmc_ffn = _pick_tile(B * Lc, 8, TM_FFN)
    lat_row = lambda i: i // (S // tm)
    lat_row_proj = lambda i: i // (S // tm_proj)
    lat_row_ffn = lambda i: i // (S // tm_ffn)
    ctx_row = lambda i: B
    n_mod_rows = -(-(B + 1) // 8) * 8

    cc = jnp.zeros((n_mod_rows, D), F32).at[:B].set(c).at[B].set(c_ctx)
    mod = _modulation(cc, w_ada, b_ada).reshape(L, n_mod_rows, 6, 1, D)

    cos_t, sin_t = _rope_tables(S)
    cos_id = jnp.ones((tmc_proj, MIX_W), F32)
    sin_id = jnp.zeros((tmc_proj, MIX_W), F32)
    lane = jnp.arange(2 * LANES)
    bd = (lane[:, None] // HEAD_DIM == lane[None, :] // HEAD_DIM).astype(BF16)
    swa_mask = _swa_mask_table()

    x2 = x.reshape(B * S, D)
    c2 = ctx.reshape(B * Lc, D)
    tn_merge = TN_MERGE

    (aq, ak, av, bq, bk, bv, sq, sk, sv, du), w_gl = _split_w_in(w_in)
    w_cat = jnp.concatenate([aq, ak, av, bq, bk, bv, sq, _dup_heads(sk), _dup_heads(sv), du], axis=-1).astype(BF16)
    w_gate_r = w_gl.astype(BF16)
    w_br = w_branch.astype(BF16)
    w_o = w_out.astype(BF16)
    wg, wu, wd = w_ffn_gate.astype(BF16), w_ffn_up.astype(BF16), w_ffn_down.astype(BF16)

    for l in range(L):
        last = l == L - 1
        lam_init = 0.8 - 0.6 * math.exp(-0.3 * l)

        pvecs = _proj_column_vectors([
            _tile_heads(a_q_norm[l], Q_SCALE), _tile_heads(a_k_norm[l]), None,
            _tile_heads(b_q_norm[l], Q_SCALE), _tile_heads(b_k_norm[l]), None,
            _tile_heads(c_q_norm[l], Q_SCALE),
            _tile_heads(c_k_norm[l]),
        ])
        g_mix_norm = norm_mix[l].reshape(1, D)
        g_ffn_norm = norm_ffn[l].reshape(1, D)
        lamp = jnp.stack([b_lam_q1[l], b_lam_k1[l], b_lam_q2[l], b_lam_k2[l]]).astype(F32)
        subln_g = b_subln[l].reshape(1, LANES).astype(F32)
        sink = c_sink[l].astype(F32)
        w_pool = d_w[l].astype(BF16)
        pool_scale = d_scale[l].reshape(1, MIX_W).astype(F32)

        qkv, du_l, h_l = _project(x2, mod, l, lat_row_proj, g_mix_norm, w_cat, pvecs, cos_t, sin_t,
                                  lambda i: i % (S // tm_proj), bd, tm_proj)
        qkv_c, du_c, h_c = _project(c2, mod, l, ctx_row, g_mix_norm, w_cat, pvecs, cos_id, sin_id,
                                    lambda i: 0, bd, tmc_proj)

        o_a = _na_attention(qkv, qkv_c, _na_bias_table(a_rpb[l], S // GRID_W), B, S, Lc)
        o_b = _diff_attention(qkv, qkv_c, lamp, subln_g, lam_init, B, S, Lc)
        o_s = _swa_attention(qkv, qkv_c, sink, swa_mask, B, S, Lc)
        o_d = _pool_mixer(du_l, w_pool, pool_scale, B, S)
        acc = _merge(h_l, (o_a, o_b, o_s, o_d), l, w_gate_r, w_br, tm_merge, tn_merge)
        x2 = _out_project(acc, l, w_o, x2, mod, lat_row, tm)

        if not last:
            oc_a, oc_b, oc_s = _ctx_attention(qkv_c, sink, lamp, subln_g, lam_init, B, Lc)
            oc_d = _pool_mixer(du_c, w_pool, pool_scale, B, Lc)
            acc_c = _merge(h_c, (oc_a, oc_b, oc_s, oc_d), l, w_gate_r, w_br, tmc, tn_merge)
            c2 = _out_project(acc_c, l, w_o, c2, mod, ctx_row, tmc)
            c2 = _ffn(c2, mod, l, ctx_row, g_ffn_norm, wg, wu, wd, tmc_ffn, TF_FFN)

        x2 = _ffn(x2, mod, l, lat_row_ffn, g_ffn_norm, wg, wu, wd, tm_ffn, TF_FFN)

    return x2.reshape(B, S, D)
```

```python
import functools
import math

import numpy as np
import jax
import jax.numpy as jnp
from jax import lax
from jax.experimental import pallas as pl
from jax.experimental.pallas import tpu as pltpu

F32 = jnp.float32
BF16 = jnp.bfloat16

HEAD_DIM = 64
LANES = 128
GRID_W = 64
N_BRANCH = 4
NA_ROWS = 8
NA_COLS = 16
SWA_WINDOW = 128
SWA_KV_HEADS = 2
POOL_WINDOWS = (2, 4, 8, 16)
ROPE_BASE = 10000.0
NORM_EPS = 1e-6
MASK_VALUE = -1e30
ATTN_SCALE = HEAD_DIM ** -0.5
LOG2E = math.log2(math.e)
Q_SCALE = ATTN_SCALE * LOG2E
MIX_W = 512
VMEM_LIMIT = 56 * 1024 * 1024

TM_PROJ = 1024
TM_FFN = 1024
TM_MERGE = 1024
TM_OUT = 512
TN_MERGE = 512
TF_FFN = 512


def _cparams(sem):
    return pltpu.CompilerParams(dimension_semantics=sem, vmem_limit_bytes=VMEM_LIMIT)


def _dot(a, b):
    return jnp.dot(a, b, preferred_element_type=F32)


def _dot_nt(a, b):
    return lax.dot_general(a, b, (((1,), (1,)), ((), ())), preferred_element_type=F32)


def _lane_lo(shape):
    return lax.broadcasted_iota(jnp.int32, shape, len(shape) - 1) % LANES < HEAD_DIM


def _stack_halves(q2):
    lo = _lane_lo(q2.shape)
    zero = jnp.zeros_like(q2)
    return jnp.concatenate([jnp.where(lo, q2, zero), jnp.where(lo, zero, q2)], axis=0)


def _merge_halves(o, t):
    lo = _lane_lo((t, LANES))
    return jnp.where(lo, o[:t], o[t:])


def _mod_kernel(c_ref, w_ref, b_ref, o_ref):
    c = c_ref[...]
    act = (c * jax.nn.sigmoid(c)).astype(BF16)
    o_ref[...] = _dot(act, w_ref[...].astype(BF16)) + b_ref[...]


def _modulation(cc, w_ada, b_ada):
    L, D, N = w_ada.shape
    R = cc.shape[0]
    tn = 1024
    return pl.pallas_call(
        _mod_kernel,
        grid=(L, N // tn),
        in_specs=[
            pl.BlockSpec((R, D), lambda l, j: (0, 0)),
            pl.BlockSpec((None, D, tn), lambda l, j: (l, 0, j)),
            pl.BlockSpec((None, 1, tn), lambda l, j: (l, 0, j)),
        ],
        out_specs=pl.BlockSpec((None, R, tn), lambda l, j: (l, 0, j)),
        out_shape=jax.ShapeDtypeStruct((L, R, N), F32),
        compiler_params=_cparams(("parallel", "parallel")),
        name="adaln_mod",
    )(cc, w_ada, b_ada.reshape(L, 1, N))


def _modnorm_into(dst_ref, x_ref, g_ref, sh_ref, sc_ref, chunk=128):
    tm = x_ref.shape[0]
    sh = sh_ref[...]
    gain = g_ref[...] * (1.0 + sc_ref[...])

    for r in range(tm // chunk):
        rows = slice(r * chunk, (r + 1) * chunk)
        xf = x_ref[rows, :]
        ms = jnp.mean(xf * xf, axis=-1, keepdims=True)
        dst_ref[rows, :] = (xf * lax.rsqrt(ms + NORM_EPS) * gain + sh).astype(dst_ref.dtype)


def _proj_kernel(x_ref, g_ref, sh_ref, sc_ref, w_ref, pv_ref, cos_ref, sin_ref, bd_ref,
                 qkv_ref, du_ref, h_ref, y_ref, *, kinds):
    j = pl.program_id(1)

    def headnorm(yp):
        y2 = (yp * yp).astype(BF16)
        hw = bd_ref.shape[0]
        parts = [_dot(y2[:, c * hw:(c + 1) * hw], bd_ref[...]) for c in range(y2.shape[1] // hw)]
        ss = parts[0] if len(parts) == 1 else jnp.concatenate(parts, axis=1)
        return yp * (lax.rsqrt(ss * (1.0 / HEAD_DIM) + NORM_EPS) * pv_ref[0:1, :yp.shape[1]])

    def rope(n):
        w = n.shape[-1]
        lane = lax.broadcasted_iota(jnp.int32, n.shape, 1)
        partner = jnp.where((lane % 32) < 16, pltpu.roll(n, w - 16, 1), pltpu.roll(n, 16, 1))
        return n * cos_ref[:, :w] + partner * sin_ref[:, :w]

    def half_norm_rope(yp):
        half = yp.shape[1] // 2
        return jnp.concatenate([rope(headnorm(yp[:, :half])), yp[:, half:]], axis=1)

    epilogues = {
        "plain": lambda yp: yp,
        "norm": headnorm,
        "norm_rope": lambda yp: rope(headnorm(yp)),
        "half_norm_rope": half_norm_rope,
    }

    def step(epilogue):
        if epilogue is not None:
            qkv_ref[...] = epilogue(y_ref[...]).astype(BF16)
        y = _dot(h_ref[...], w_ref[...])
        y_ref[...] = y

        @pl.when(j == pl.num_programs(1) - 1)
        def _():
            du_ref[...] = y

    @pl.when(j == 0)
    def _():
        _modnorm_into(h_ref, x_ref, g_ref, sh_ref, sc_ref)
        step(None)

    for kind, epilogue in epilogues.items():
        steps = [b + 1 for b, k in enumerate(kinds) if k == kind]
        if steps:
            pl.when(functools.reduce(lambda a, b: a | b, [j == s for s in steps]))(
                functools.partial(step, epilogue))


_QKV_BLOCK_KINDS = ("norm", "norm", "plain", "norm_rope", "norm_rope", "plain", "norm_rope", "half_norm_rope")


def _project(x2, mod5, l, row_of_tile, g_norm, w_cat, pvecs, cos_t, sin_t, table_block, bd, tm):
    M, D = x2.shape
    nblk = w_cat.shape[2] // MIX_W
    nq = nblk - 1
    mod_spec = lambda k: pl.BlockSpec((None, None, None, 1, D), lambda i, j: (l, row_of_tile(i), k, 0, 0))
    assert len(_QKV_BLOCK_KINDS) == nq
    return pl.pallas_call(
        functools.partial(_proj_kernel, kinds=_QKV_BLOCK_KINDS),
        grid=(M // tm, nblk),
        in_specs=[
            pl.BlockSpec((tm, D), lambda i, j: (i, 0)),
            pl.BlockSpec((1, D), lambda i, j: (0, 0)),
            mod_spec(0), mod_spec(1),
            pl.BlockSpec((None, D, MIX_W), lambda i, j: (l, 0, j)),
            pl.BlockSpec((None, 8, MIX_W), lambda i, j: (jnp.maximum(j - 1, 0), 0, 0)),
            pl.BlockSpec((tm, MIX_W), lambda i, j: (table_block(i), 0)),
            pl.BlockSpec((tm, MIX_W), lambda i, j: (table_block(i), 0)),
            pl.BlockSpec(bd.shape, lambda i, j: (0, 0)),
        ],
        out_specs=[
            pl.BlockSpec((tm, MIX_W), lambda i, j: (i, jnp.maximum(j - 1, 0))),
            pl.BlockSpec((tm, MIX_W), lambda i, j: (i, 0)),
            pl.BlockSpec((tm, D), lambda i, j: (i, 0)),
        ],
        out_shape=[
            jax.ShapeDtypeStruct((M, nq * MIX_W), BF16),
            jax.ShapeDtypeStruct((M, MIX_W), F32),
            jax.ShapeDtypeStruct((M, D), BF16),
        ],
        scratch_shapes=[pltpu.VMEM((tm, MIX_W), F32)],
        compiler_params=_cparams(("parallel", "arbitrary")),
        name="in_proj",
    )(x2, g_norm, mod5, mod5, w_cat, pvecs, cos_t, sin_t, bd)


_QA, _KA, _VA = 0, 4, 8
_QB, _KB, _VB = 12, 16, 20
_QC = 24
_KC, _VC = 28, 30


def _softmax_pv(parts, extra_logit=None):
    s = jnp.concatenate([s for s, _ in parts], axis=1) if len(parts) > 1 else parts[0][0]
    m = s.max(axis=-1, keepdims=True)
    if extra_logit is not None:
        m = jnp.maximum(m, extra_logit)
    p = jnp.exp2(s - m)
    l = p.sum(axis=-1, keepdims=True)
    if extra_logit is not None:
        l = l + jnp.exp2(extra_logit - m)
    pb = p.astype(BF16)
    o = None
    start = 0
    for sp, v in parts:
        width = sp.shape[1]
        term = _dot(pb[:, start:start + width], v)
        o = term if o is None else o + term
        start += width
    return o / l


NA_GROUP = 4
NA_UNION = NA_GROUP + NA_ROWS


def _na_kernel(q_ref, k_ref, v_ref, kc_ref, vc_ref, bias_ref, o_ref, *, rows, units):
    g0 = pl.program_id(2) * units
    n_groups = rows // NA_GROUP
    tq = NA_GROUP * GRID_W
    kc = kc_ref[...]
    vc = vc_ref[...]
    scores = []
    for u in range(units):
        g = g0 + u
        start = jnp.clip(NA_GROUP * g - NA_ROWS // 2, 0, rows - NA_UNION)
        variant = jnp.where(g == 0, 0, jnp.where(g == n_groups - 1, 2, 1))
        krows = pl.ds(pl.multiple_of(start * GRID_W, GRID_W), NA_UNION * GRID_W)
        qs = _stack_halves(q_ref[u * tq:(u + 1) * tq, :])
        s = _dot_nt(qs, k_ref[krows, :]) + bias_ref[variant]
        sc = _dot_nt(qs, kc)
        scores.append((s, sc, krows))
    for u, (s, sc, krows) in enumerate(scores):
        o = _softmax_pv([(s, v_ref[krows, :]), (sc, vc)])
        o_ref[u * tq:(u + 1) * tq, :] = _merge_halves(o, tq).astype(o_ref.dtype)


def _na_attention(qkv, qkv_c, bias, B, S, Lc):
    rows = S // GRID_W
    assert rows % NA_GROUP == 0 and rows >= NA_UNION
    n_groups = rows // NA_GROUP
    units = 8 if n_groups % 8 == 0 else (4 if n_groups % 4 == 0 else 1)
    nstep = n_groups // units
    tq = units * NA_GROUP * GRID_W
    npair = MIX_W // LANES
    return pl.pallas_call(
        functools.partial(_na_kernel, rows=rows, units=units),
        grid=(B, npair, nstep),
        in_specs=[
            pl.BlockSpec((tq, LANES), lambda b, p, r: (b * nstep + r, _QA + p)),
            pl.BlockSpec((S, LANES), lambda b, p, r: (b, _KA + p)),
            pl.BlockSpec((S, LANES), lambda b, p, r: (b, _VA + p)),
            pl.BlockSpec((Lc, LANES), lambda b, p, r: (b, _KA + p)),
            pl.BlockSpec((Lc, LANES), lambda b, p, r: (b, _VA + p)),
            pl.BlockSpec((None,) + bias.shape[1:], lambda b, p, r: (p, 0, 0, 0)),
        ],
        out_specs=pl.BlockSpec((tq, LANES), lambda b, p, r: (b * nstep + r, p)),
        out_shape=jax.ShapeDtypeStruct((B * S, MIX_W), BF16),
        compiler_params=_cparams(("parallel", "parallel", "arbitrary")),
        name="na_attention",
    )(qkv, qkv, qkv, qkv_c, qkv_c, bias)


def _na_bias_kernel(rpb_ref, onehot_ref, valid_ref, o_ref):
    r = rpb_ref[...]
    hi = r.astype(BF16)
    r1 = r - hi.astype(F32)
    mid = r1.astype(BF16)
    lo = (r1 - mid.astype(F32)).astype(BF16)
    oh = onehot_ref[...]
    t = _dot(hi, oh) + _dot(mid, oh) + _dot(lo, oh)
    o_ref[...] = jnp.where(valid_ref[...] > 0.0, t * LOG2E, MASK_VALUE)


def _na_window_pattern(rows, g):
    start_u = np.clip(NA_GROUP * g - NA_ROWS // 2, 0, rows - NA_UNION)
    r = NA_GROUP * g + np.arange(NA_GROUP)[:, None]
    start_r = np.clip(r - NA_ROWS // 2, 0, rows - NA_ROWS)
    kr = start_u + np.arange(NA_UNION)[None, :]
    valid = (kr >= start_r) & (kr < start_r + NA_ROWS)
    return np.where(valid, kr - r + (NA_ROWS - 1), 0), valid


def _na_bias_table(rpb, rows):
    H, n_dr, n_dc = rpb.shape
    n_groups = rows // NA_GROUP
    patterns = [_na_window_pattern(rows, g) for g in (0, 1, n_groups - 1)]
    for g in range(1, n_groups - 1):
        dr_g, valid_g = _na_window_pattern(rows, g)
        assert (dr_g == patterns[1][0]).all() and (valid_g == patterns[1][1]).all()
    col = np.arange(GRID_W)
    col_start = np.clip(col - NA_COLS // 2, 0, GRID_W - NA_COLS)
    valid = (col[None, :] >= col_start[:, None]) & (col[None, :] < col_start[:, None] + NA_COLS)
    dc = np.clip(col[None, :] - col[:, None], -(NA_COLS - 1), NA_COLS - 1) + (NA_COLS - 1)
    onehot = (np.arange(LANES)[:, None] == dc.reshape(1, -1)).astype(np.float32)
    n_rows = -(-(H * n_dr) // LANES) * LANES
    rpb_rows = jnp.zeros((n_rows, LANES), F32).at[:H * n_dr, :n_dc].set(rpb.reshape(H * n_dr, n_dc).astype(F32))
    t = pl.pallas_call(
        _na_bias_kernel,
        out_shape=jax.ShapeDtypeStruct((n_rows, GRID_W * GRID_W), F32),
        name="na_bias",
    )(rpb_rows, jnp.asarray(onehot, BF16), jnp.asarray(valid.reshape(1, -1), F32))
    t3 = t.reshape(n_rows, GRID_W, GRID_W)
    out_rows, out_cols = 2 * NA_GROUP * GRID_W, NA_UNION * GRID_W
    return pl.pallas_call(
        functools.partial(_na_assemble_kernel, patterns=patterns, n_dr=n_dr),
        grid=(H // 2,),
        in_specs=[pl.BlockSpec(t3.shape, lambda p: (0, 0, 0))],
        out_specs=pl.BlockSpec((None, len(patterns), out_rows, out_cols), lambda p: (p, 0, 0, 0)),
        out_shape=jax.ShapeDtypeStruct((H // 2, len(patterns), out_rows, out_cols), F32),
        compiler_params=_cparams(("parallel",)),
        name="na_bias_assemble",
    )(t3)


def _na_assemble_kernel(t3_ref, o_ref, *, patterns, n_dr):
    p = pl.program_id(0)
    masked = jnp.full((GRID_W, GRID_W), MASK_VALUE, F32)
    for v, (dr, ok) in enumerate(patterns):
        for parity in range(2):
            base = (2 * p + parity) * n_dr
            for rr in range(NA_GROUP):
                r0 = (parity * NA_GROUP + rr) * GRID_W
                for j in range(NA_UNION):
                    tile = t3_ref[base + int(dr[rr, j])] if ok[rr, j] else masked
                    o_ref[v, r0:r0 + GRID_W, j * GRID_W:(j + 1) * GRID_W] = tile


def _lambda_full(lamp_ref, lam_init):
    lp = lamp_ref[...]
    a = jnp.sum(lp[0:1] * lp[1:2], axis=-1, keepdims=True)
    b = jnp.sum(lp[2:3] * lp[3:4], axis=-1, keepdims=True)
    return jnp.exp(a) - jnp.exp(b) + lam_init


def _subln(o, g, lam_init):
    ms = jnp.mean(o * o, axis=-1, keepdims=True)
    return o * lax.rsqrt(ms + NORM_EPS) * g * (1.0 - lam_init)


def _fold_lanes(x, op):
    out = x[:, :LANES]
    for b in range(1, x.shape[1] // LANES):
        out = op(out, x[:, b * LANES:(b + 1) * LANES])
    return out


def _diff_kernel(q_ref, k_ref, v_ref, kc_ref, vc_ref, lamp_ref, g_ref, o_ref, s_ref, *, lam_init, tk):
    tq = q_ref.shape[0]
    S = k_ref.shape[0]
    nkb = S // tk
    lam = _lambda_full(lamp_ref, lam_init)
    nh = q_ref.shape[1] // LANES
    heads = [slice(h * LANES, (h + 1) * LANES) for h in range(nh)]
    qs = [_stack_halves(q_ref[:, hs]) for hs in heads]
    lane_max, m, lane_sum, acc = [None] * nh, [None] * nh, [None] * nh, [None] * nh

    def pass1_block(h, kb):
        s = _dot_nt(qs[h], k_ref[kb * tk:(kb + 1) * tk, heads[h]])
        s_ref[h, kb] = s
        blk = _fold_lanes(s, jnp.maximum)
        lane_max[h] = blk if lane_max[h] is None else jnp.maximum(lane_max[h], blk)

    pc, coef = [None] * nh, [None] * nh

    def pass1_finish(h):
        sc = _dot_nt(qs[h], kc_ref[:, heads[h]])
        m[h] = jnp.maximum(lane_max[h], _fold_lanes(sc, jnp.maximum)).max(axis=-1, keepdims=True)
        pc[h] = jnp.exp2(sc - m[h])
        lane_sum[h] = _fold_lanes(pc[h], jnp.add)

    def pass2_block(h, kb):
        p = jnp.exp2(s_ref[h, kb] - m[h])
        s_ref[h, kb] = p
        lane_sum[h] = lane_sum[h] + _fold_lanes(p, jnp.add)

    def pass2_finish(h):
        l = lane_sum[h].sum(axis=-1, keepdims=True)
        coef[h] = (-lam * l[:tq] / l[tq:], 1.0 / l[:tq])

    def combined(p, h):
        return (p[:tq] + coef[h][0] * p[tq:]).astype(BF16)

    def pass3_block(h, kb):
        part = _dot(combined(s_ref[h, kb], h), v_ref[kb * tk:(kb + 1) * tk, heads[h]])
        acc[h] = part if acc[h] is None else acc[h] + part

    def pass3_finish(h):
        o = (acc[h] + _dot(combined(pc[h], h), vc_ref[:, heads[h]])) * coef[h][1]
        o_ref[:, heads[h]] = _subln(o, g_ref[...], lam_init).astype(o_ref.dtype)

    for t in range(nh + 2):
        for kb in range(nkb):
            if t < nh:
                pass1_block(t, kb)
            if 1 <= t <= nh:
                pass2_block(t - 1, kb)
            if t >= 2:
                pass3_block(t - 2, kb)
        if t < nh:
            pass1_finish(t)
        if 1 <= t <= nh:
            pass2_finish(t - 1)
        if t >= 2:
            pass3_finish(t - 2)


def _diff_attention(qkv, qkv_c, lamp, subln_g, lam_init, B, S, Lc):
    tq = 256
    nq = S // tq
    hw = 4 * LANES
    ngrp = MIX_W // hw
    qb, kb, vb = _QB * LANES // hw, _KB * LANES // hw, _VB * LANES // hw
    tk = 512
    nh = hw // LANES
    return pl.pallas_call(
        functools.partial(_diff_kernel, lam_init=lam_init, tk=tk),
        grid=(B, ngrp, nq),
        in_specs=[
            pl.BlockSpec((tq, hw), lambda b, h, i: (b * nq + i, qb + h)),
            pl.BlockSpec((S, hw), lambda b, h, i: (b, kb + h)),
            pl.BlockSpec((S, hw), lambda b, h, i: (b, vb + h)),
            pl.BlockSpec((Lc, hw), lambda b, h, i: (b, kb + h)),
            pl.BlockSpec((Lc, hw), lambda b, h, i: (b, vb + h)),
            pl.BlockSpec((4, HEAD_DIM), lambda b, h, i: (0, 0)),
            pl.BlockSpec((1, LANES), lambda b, h, i: (0, 0)),
        ],
        out_specs=pl.BlockSpec((tq, hw), lambda b, h, i: (b * nq + i, h)),
        out_shape=jax.ShapeDtypeStruct((B * S, MIX_W), BF16),
        scratch_shapes=[pltpu.VMEM((nh, S // tk, 2 * tq, tk), F32)],
        compiler_params=_cparams(("parallel", "parallel", "arbitrary")),
        name="diff_attention",
    )(qkv, qkv, qkv, qkv_c, qkv_c, lamp, subln_g)


def _stack_group(q4):
    return jnp.concatenate([_stack_halves(q4[:, :LANES]), _stack_halves(q4[:, LANES:])], axis=0)


def _merge_group(o, t):
    return jnp.concatenate([_merge_halves(o[:2 * t], t), _merge_halves(o[2 * t:], t)], axis=-1)


def _sink_column(sink_ref, c, t):
    group = sink_ref.shape[0] // SWA_KV_HEADS
    return jnp.concatenate([jnp.full((t, 1), sink_ref[c * group + g] * LOG2E, F32) for g in range(group)], axis=0)


def _swa_kernel(sink_ref, q_ref, k_ref, v_ref, kc_ref, vc_ref, mask_ref, o_ref, *, qb):
    c = pl.program_id(1)
    n0 = pl.program_id(2) * qb
    W = SWA_WINDOW
    S = k_ref.shape[0]
    nb = S // W
    kc = kc_ref[...]
    vc = vc_ref[...]
    sink_col = _sink_column(sink_ref, c, W)
    group = sink_ref.shape[0] // SWA_KV_HEADS
    scores = []
    for t in range(qb):
        n = n0 + t
        ks = pl.multiple_of(jnp.clip((n - 1) * W, 0, S - 3 * W), W)
        krows = pl.ds(ks, 3 * W)
        variant = jnp.where(n == 0, 0, jnp.where(n == nb - 1, 2, 1))
        mask = mask_ref[variant]
        qs = _stack_group(q_ref[t * W:(t + 1) * W, :])
        s = _dot_nt(qs, k_ref[krows, :]) + jnp.concatenate([mask] * group, axis=0)
        sc = _dot_nt(qs, kc)
        scores.append((s, sc, krows))
    for t, (s, sc, krows) in enumerate(scores):
        o = _softmax_pv([(s, v_ref[krows, :]), (sc, vc)], extra_logit=sink_col)
        o_ref[t * W:(t + 1) * W, :] = _merge_group(o, W).astype(o_ref.dtype)


def _swa_mask_table():
    W = SWA_WINDOW
    i = np.arange(W)[:, None]
    j = np.arange(3 * W)[None, :]
    tabs = [np.where(np.abs(j - shift - i) <= W, 0.0, MASK_VALUE) for shift in (0, W, 2 * W)]
    return jnp.asarray(np.stack(tabs), F32)


def _swa_attention(qkv, qkv_c, sink, mask, B, S, Lc):
    W = SWA_WINDOW
    nb = S // W
    qb = 16 if nb % 16 == 0 else (8 if nb % 8 == 0 else 1)
    nqb = nb // qb
    return pl.pallas_call(
        functools.partial(_swa_kernel, qb=qb),
        grid=(B, SWA_KV_HEADS, nqb),
        in_specs=[
            pl.BlockSpec(memory_space=pltpu.SMEM),
            pl.BlockSpec((qb * W, 2 * LANES), lambda b, c, n: (b * nqb + n, _QC // 2 + c)),
            pl.BlockSpec((S, LANES), lambda b, c, n: (b, _KC + c)),
            pl.BlockSpec((S, LANES), lambda b, c, n: (b, _VC + c)),
            pl.BlockSpec((Lc, LANES), lambda b, c, n: (b, _KC + c)),
            pl.BlockSpec((Lc, LANES), lambda b, c, n: (b, _VC + c)),
            pl.BlockSpec(mask.shape, lambda b, c, n: (0, 0, 0)),
        ],
        out_specs=pl.BlockSpec((qb * W, 2 * LANES), lambda b, c, n: (b * nqb + n, c)),
        out_shape=jax.ShapeDtypeStruct((B * S, MIX_W), BF16),
        compiler_params=_cparams(("parallel", "parallel", "arbitrary")),
        name="swa_attention",
    )(sink, qkv, qkv, qkv, qkv_c, qkv_c, mask)


def _ctx_attn_kernel(sink_ref, qkv_ref, lamp_ref, g_ref, oa_ref, ob_ref, os_ref, *, lam_init):
    Lc = qkv_ref.shape[0]

    def cols(blk, width=LANES):
        return qkv_ref[:, blk * LANES: blk * LANES + width]

    for p in range(MIX_W // LANES):
        qs = _stack_halves(cols(_QA + p))
        o = _softmax_pv([(_dot_nt(qs, cols(_KA + p)), cols(_VA + p))])
        oa_ref[:, p * LANES:(p + 1) * LANES] = _merge_halves(o, Lc).astype(oa_ref.dtype)

    lam = _lambda_full(lamp_ref, lam_init)
    for h in range(MIX_W // LANES):
        qs = _stack_halves(cols(_QB + h))
        on = _softmax_pv([(_dot_nt(qs, cols(_KB + h)), cols(_VB + h))])
        o = on[:Lc] - lam * on[Lc:]
        ob_ref[:, h * LANES:(h + 1) * LANES] = _subln(o, g_ref[...], lam_init).astype(ob_ref.dtype)

    for c in range(SWA_KV_HEADS):
        qs = _stack_group(cols(_QC + 2 * c, 2 * LANES))
        o = _softmax_pv([(_dot_nt(qs, cols(_KC + c)), cols(_VC + c))],
                        extra_logit=_sink_column(sink_ref, c, Lc))
        os_ref[:, c * 2 * LANES:(c + 1) * 2 * LANES] = _merge_group(o, Lc).astype(os_ref.dtype)


def _ctx_attention(qkv_c, sink, lamp, subln_g, lam_init, B, Lc):
    NQ = qkv_c.shape[1]
    out = jax.ShapeDtypeStruct((B * Lc, MIX_W), BF16)
    ospec = pl.BlockSpec((Lc, MIX_W), lambda b: (b, 0))
    return pl.pallas_call(
        functools.partial(_ctx_attn_kernel, lam_init=lam_init),
        grid=(B,),
        in_specs=[
            pl.BlockSpec(memory_space=pltpu.SMEM),
            pl.BlockSpec((Lc, NQ), lambda b: (b, 0)),
            pl.BlockSpec((4, HEAD_DIM), lambda b: (0, 0)),
            pl.BlockSpec((1, LANES), lambda b: (0, 0)),
        ],
        out_specs=[ospec, ospec, ospec],
        out_shape=[out, out, out],
        compiler_params=_cparams(("parallel",)),
        name="ctx_attention",
    )(sink, qkv_c, lamp, subln_g)


def _shift_down(x, k, t_idx):
    return jnp.where(t_idx >= k, pltpu.roll(x, k, 0), 0.0)


def _shift_up(x, k, t_idx):
    T = x.shape[0]
    return jnp.where(t_idx < T - k, pltpu.roll(x, T - k, 0), 0.0)


def _pool_kernel(u_ref, w_ref, scale_ref, o_ref):
    T = u_ref.shape[0]
    pg = u_ref.shape[1] // len(POOL_WINDOWS)
    t_idx = lax.broadcasted_iota(jnp.int32, (T, pg), 0)
    for g, w in enumerate(POOL_WINDOWS):
        u = u_ref[:, g * pg:(g + 1) * pg]
        back = u
        fwd = u
        span = 1
        while span < w // 2:
            back = back + _shift_down(back, span, t_idx)
            fwd = fwd + _shift_up(fwd, span, t_idx)
            span *= 2
        win = _shift_down(back, 1, t_idx) + fwd
        lo = jnp.clip(t_idx - w // 2, 0, T)
        hi = jnp.clip(t_idx - w // 2 + w, 0, T)
        d = (win / (hi - lo).astype(F32) - u).astype(BF16)
        mixed = _dot(d, w_ref[g])
        o_ref[:, g * pg:(g + 1) * pg] = (mixed * scale_ref[:, g * pg:(g + 1) * pg]).astype(o_ref.dtype)


def _pool_mixer(du, w_grp, scale, nseq, T):
    return pl.pallas_call(
        _pool_kernel,
        grid=(nseq,),
        in_specs=[
            pl.BlockSpec((T, MIX_W), lambda b: (b, 0)),
            pl.BlockSpec(w_grp.shape, lambda b: (0, 0, 0)),
            pl.BlockSpec((1, MIX_W), lambda b: (0, 0)),
        ],
        out_specs=pl.BlockSpec((T, MIX_W), lambda b: (b, 0)),
        out_shape=jax.ShapeDtypeStruct((nseq * T, MIX_W), BF16),
        compiler_params=_cparams(("parallel",)),
        name="pool_mixer",
    )(du, w_grp, scale)


def _merge_kernel(h_ref, oa_ref, ob_ref, os_ref, od_ref, wg0_ref, wg1_ref, wg2_ref, wg3_ref, wb_ref,
                  acc_ref, f32_ref):
    h = h_ref[...]
    branches = zip((oa_ref, ob_ref, os_ref, od_ref), (wg0_ref, wg1_ref, wg2_ref, wg3_ref))
    for n, (o_ref, wg_ref) in enumerate(branches):
        gate = jax.nn.sigmoid(_dot(h, wg_ref[...]))
        term = gate * _dot(o_ref[...], wb_ref[n])
        if n == 0:
            f32_ref[...] = term
        else:
            f32_ref[...] += term
    acc_ref[...] = f32_ref[...].astype(acc_ref.dtype)


def _merge(h, outs, l, w_gate, w_br, tm, tn):
    M, D = h.shape
    nj = D // tn
    ospec = pl.BlockSpec((tm, MIX_W), lambda i, j: (i, 0))
    gate_spec = lambda n: pl.BlockSpec((None, D, tn), lambda i, j: (l, 0, n * nj + j))
    return pl.pallas_call(
        _merge_kernel,
        grid=(M // tm, nj),
        in_specs=[
            pl.BlockSpec((tm, D), lambda i, j: (i, 0)),
            ospec, ospec, ospec, ospec,
            gate_spec(0), gate_spec(1), gate_spec(2), gate_spec(3),
            pl.BlockSpec((None, N_BRANCH, MIX_W, tn), lambda i, j: (l, 0, 0, j)),
        ],
        out_specs=pl.BlockSpec((tm, tn), lambda i, j: (i, j)),
        out_shape=jax.ShapeDtypeStruct((M, D), BF16),
        scratch_shapes=[pltpu.VMEM((tm, tn), F32)],
        compiler_params=_cparams(("parallel", "arbitrary")),
        name="gated_merge",
    )(h, *outs, w_gate, w_gate, w_gate, w_gate, w_br)


def _outproj_kernel(a_ref, w_ref, x_ref, g_ref, o_ref, *, tc):
    a = a_ref[...]
    for c in range(o_ref.shape[1] // tc):
        cs = slice(c * tc, (c + 1) * tc)
        o_ref[:, cs] = x_ref[:, cs] + g_ref[:, cs] * _dot(a, w_ref[:, cs])


def _out_project(acc, l, w_o, x2, mod5, row_of_tile, tm):
    M, D = x2.shape
    return pl.pallas_call(
        functools.partial(_outproj_kernel, tc=512),
        grid=(M // tm,),
        in_specs=[
            pl.BlockSpec((tm, D), lambda i: (i, 0)),
            pl.BlockSpec((None, D, D), lambda i: (l, 0, 0), pipeline_mode=pl.Buffered(1)),
            pl.BlockSpec((tm, D), lambda i: (i, 0)),
            pl.BlockSpec((None, None, None, 1, D), lambda i: (l, row_of_tile(i), 2, 0, 0)),
        ],
        out_specs=pl.BlockSpec((tm, D), lambda i: (i, 0)),
        out_shape=jax.ShapeDtypeStruct((M, D), F32),
        compiler_params=_cparams(("parallel",)),
        name="out_proj",
    )(acc, w_o, x2, mod5)


def _ffn_kernel(x_ref, g_ref, sh_ref, sc_ref, gate_ref, wg_ref, wu_ref, wd_ref, o_ref, h_ref, *, tc, nf):
    f = pl.program_id(1)
    D = o_ref.shape[1]

    last = nf - 1

    def step(first, final):
        if first:
            _modnorm_into(h_ref, x_ref, g_ref, sh_ref, sc_ref)
        h = h_ref[...]
        a = _dot(h, wg_ref[...])
        u = _dot(h, wu_ref[...])
        act = (a * jax.nn.sigmoid(a) * u).astype(BF16)
        for c in range(D // tc):
            cs = slice(c * tc, (c + 1) * tc)
            part = _dot(act, wd_ref[:, cs])
            total = part if first else o_ref[:, cs] + part
            o_ref[:, cs] = x_ref[:, cs] + gate_ref[:, cs] * total if final else total

    if nf == 1:
        step(True, True)
        return

    @pl.when(f == 0)
    def _():
        step(True, False)

    @pl.when((f > 0) & (f < last))
    def _():
        step(False, False)

    @pl.when(f == last)
    def _():
        step(False, True)


def _ffn(x2, mod5, l, row_of_tile, g_norm, wg, wu, wd, tm, tf):
    M, D = x2.shape
    F = wg.shape[2]
    mod_spec = lambda k: pl.BlockSpec((None, None, None, 1, D), lambda i, f: (l, row_of_tile(i), k, 0, 0))
    return pl.pallas_call(
        functools.partial(_ffn_kernel, tc=512, nf=F // tf),
        grid=(M // tm, F // tf),
        in_specs=[
            pl.BlockSpec((tm, D), lambda i, f: (i, 0)),
            pl.BlockSpec((1, D), lambda i, f: (0, 0)),
            mod_spec(3), mod_spec(4), mod_spec(5),
            pl.BlockSpec((None, D, tf), lambda i, f: (l, 0, f)),
            pl.BlockSpec((None, D, tf), lambda i, f: (l, 0, f)),
            pl.BlockSpec((None, tf, D), lambda i, f: (l, f, 0)),
        ],
        out_specs=pl.BlockSpec((tm, D), lambda i, f: (i, 0)),
        out_shape=jax.ShapeDtypeStruct((M, D), F32),
        scratch_shapes=[pltpu.VMEM((tm, D), BF16)],
        compiler_params=_cparams(("parallel", "arbitrary")),
        name="swiglu_ffn",
    )(x2, g_norm, mod5, mod5, mod5, wg, wu, wd)


def _rope_tables(S):
    t = jnp.arange(S)
    pos = jnp.stack([t // GRID_W, t % GRID_W], axis=-1).astype(F32)
    n_freq = HEAD_DIM // 4
    inv = ROPE_BASE ** (-jnp.arange(n_freq, dtype=F32) / n_freq)
    ang = pos[:, :, None] * inv
    cos, sin = jnp.cos(ang), jnp.sin(ang)
    cos_h = jnp.concatenate([cos, cos], axis=-1).reshape(S, HEAD_DIM)
    sin_h = jnp.concatenate([-sin, sin], axis=-1).reshape(S, HEAD_DIM)
    reps = MIX_W // HEAD_DIM
    return jnp.tile(cos_h, (1, reps)), jnp.tile(sin_h, (1, reps))


def _split_w_in(w_in):
    sizes = (MIX_W,) * 7 + (SWA_KV_HEADS * HEAD_DIM,) * 2 + (MIX_W,)
    offs = [0]
    for s in sizes:
        offs.append(offs[-1] + s)
    parts = [w_in[..., offs[i]:offs[i + 1]] for i in range(len(sizes))]
    return parts, w_in[..., offs[-1]:]


def _dup_heads(w):
    lead = w.shape[:-1]
    w4 = jnp.repeat(w.reshape(lead + (SWA_KV_HEADS, 1, HEAD_DIM)), 2, axis=-2)
    return w4.reshape(lead + (2 * SWA_KV_HEADS * HEAD_DIM,))


def _tile_heads(g, scale=1.0):
    return jnp.tile(g.astype(F32) * scale, MIX_W // HEAD_DIM)


def _proj_column_vectors(gains):
    rows = []
    for g in gains:
        gain = jnp.ones((MIX_W,), F32) if g is None else g
        rows.append(jnp.stack([gain] + [jnp.zeros((MIX_W,), F32)] * 7))
    return jnp.stack(rows)


def _pick_tile(M, unit, target):
    t = min(target, M)
    while M % t or (t > unit and t % unit) or (t < unit and unit % t):
        t //= 2
    return t


def kernel(x, c, ctx, c_ctx, w_ada, b_ada, norm_mix, norm_ffn, w_in, a_q_norm, a_k_norm, a_rpb, b_q_norm, b_k_norm, b_lam_q1, b_lam_k1, b_lam_q2, b_lam_k2, b_subln, c_q_norm, c_k_norm, c_sink, d_w, d_scale, w_branch, w_out, w_ffn_gate, w_ffn_up, w_ffn_down):
    B, S, D = x.shape
    Lc = ctx.shape[1]
    L = w_ada.shape[0]
    assert D == N_BRANCH * MIX_W and S % GRID_W == 0 and S // GRID_W >= NA_ROWS and S >= 3 * SWA_WINDOW

    tm = _pick_tile(S, S, TM_OUT)
    tm_proj = _pick_tile(S, S, TM_PROJ)
    tm_ffn = _pick_tile(S, S, TM_FFN)
    tm_merge = _pick_tile(B * S, 8, TM_MERGE)
    tmc = _pick_tile(B * Lc, 8, TM_OUT)
    tmc_proj = _pick_tile(B * Lc, 8, TM_PROJ)
    tmc_ffn = _pick_tile(B * Lc, 8, TM_FFN)
    lat_row = lambda i: i // (S // tm)
    lat_row_proj = lambda i: i // (S // tm_proj)
    lat_row_ffn = lambda i: i // (S // tm_ffn)
    ctx_row = lambda i: B
    n_mod_rows = -(-(B + 1) // 8) * 8

    cc = jnp.zeros((n_mod_rows, D), F32).at[:B].set(c).at[B].set(c_ctx)
    mod = _modulation(cc, w_ada, b_ada).reshape(L, n_mod_rows, 6, 1, D)

    cos_t, sin_t = _rope_tables(S)
    cos_id = jnp.ones((tmc_proj, MIX_W), F32)
    sin_id = jnp.zeros((tmc_proj, MIX_W), F32)
    lane = jnp.arange(2 * LANES)
    bd = (lane[:, None] // HEAD_DIM == lane[None, :] // HEAD_DIM).astype(BF16)
    swa_mask = _swa_mask_table()

    x2 = x.reshape(B * S, D)
    c2 = ctx.reshape(B * Lc, D)
    tn_merge = TN_MERGE

    (aq, ak, av, bq, bk, bv, sq, sk, sv, du), w_gl = _split_w_in(w_in)
    w_cat = jnp.concatenate([aq, ak, av, bq, bk, bv, sq, _dup_heads(sk), _dup_heads(sv), du], axis=-1).astype(BF16)
    w_gate_r = w_gl.astype(BF16)
    w_br = w_branch.astype(BF16)
    w_o = w_out.astype(BF16)
    wg, wu, wd = w_ffn_gate.astype(BF16), w_ffn_up.astype(BF16), w_ffn_down.astype(BF16)

    for l in range(L):
        last = l == L - 1
        lam_init = 0.8 - 0.6 * math.exp(-0.3 * l)

        pvecs = _proj_column_vectors([
            _tile_heads(a_q_norm[l], Q_SCALE), _tile_heads(a_k_norm[l]), None,
            _tile_heads(b_q_norm[l], Q_SCALE), _tile_heads(b_k_norm[l]), None,
            _tile_heads(c_q_norm[l], Q_SCALE),
            _tile_heads(c_k_norm[l]),
        ])
        g_mix_norm = norm_mix[l].reshape(1, D)
        g_ffn_norm = norm_ffn[l].reshape(1, D)
        lamp = jnp.stack([b_lam_q1[l], b_lam_k1[l], b_lam_q2[l], b_lam_k2[l]]).astype(F32)
        subln_g = b_subln[l].reshape(1, LANES).astype(F32)
        sink = c_sink[l].astype(F32)
        w_pool = d_w[l].astype(BF16)
        pool_scale = d_scale[l].reshape(1, MIX_W).astype(F32)

        qkv, du_l, h_l = _project(x2, mod, l, lat_row_proj, g_mix_norm, w_cat, pvecs, cos_t, sin_t,
                                  lambda i: i % (S // tm_proj), bd, tm_proj)
        qkv_c, du_c, h_c = _project(c2, mod, l, ctx_row, g_mix_norm, w_cat, pvecs, cos_id, sin_id,
                                    lambda i: 0, bd, tmc_proj)

        o_a = _na_attention(qkv, qkv_c, _na_bias_table(a_rpb[l], S // GRID_W), B, S, Lc)
        o_b = _diff_attention(qkv, qkv_c, lamp, subln_g, lam_init, B, S, Lc)
        o_s = _swa_attention(qkv, qkv_c, sink, swa_mask, B, S, Lc)
        o_d = _pool_mixer(du_l, w_pool, pool_scale, B, S)
        acc = _merge(h_l, (o_a, o_b, o_s, o_d), l, w_gate_r, w_br, tm_merge, tn_merge)
        x2 = _out_project(acc, l, w_o, x2, mod, lat_row, tm)

        if not last:
            oc_a, oc_b, oc_s = _ctx_attention(qkv_c, sink, lamp, subln_g, lam_init, B, Lc)
            oc_d = _pool_mixer(du_c, w_pool, pool_scale, B, Lc)
            acc_c = _merge(h_c, (oc_a, oc_b, oc_s, oc_d), l, w_gate_r, w_br, tmc, tn_merge)
            c2 = _out_project(acc_c, l, w_o, c2, mod, ctx_row, tmc)
            c2 = _ffn(c2, mod, l, ctx_row, g_ffn_norm, wg, wu, wd, tmc_ffn, TF_FFN)

        x2 = _ffn(x2, mod, l, lat_row_ffn, g_ffn_norm, wg, wu, wd, tm_ffn, TF_FFN)

    return x2.reshape(B, S, D)
```
